```python
import math
import jax, jax.numpy as jnp
from jax import lax
import numpy as np

D_MODEL = 2048
BATCH = 4
SEQ = 2048
DEPTH = 1
DEC_BATCH = 128
DEC_SEQ = 8
PAST_LEN = 16384
PAGE_SIZE = 128

D_MIX = D_MODEL
D_RET = D_MIX // 2
D_CONV = D_MIX - D_RET
RET_DK = 256
RET_DV = 256
N_RET_HEADS = D_RET // RET_DV
RET_CHUNK = 128
CONV_WIDTH = 31
D_FF = 5632
ROPE_BASE = 10000.0
EPS = 1e-6
D_IN = 2 * N_RET_HEADS * RET_DK + 2 * D_RET + 2 * D_CONV

kernel_name = "hymba_conformer_retnet_decode_step"


def _rmsnorm(x, g):
    xf = x.astype(jnp.float32)
    y = xf * lax.rsqrt(jnp.mean(xf * xf, axis=-1, keepdims=True) + EPS)
    return (y * g.astype(jnp.float32)).astype(x.dtype)


def _layernorm(x, g, b):
    xf = x.astype(jnp.float32)
    mu = jnp.mean(xf, axis=-1, keepdims=True)
    var = jnp.mean(jnp.square(xf - mu), axis=-1, keepdims=True)
    y = (xf - mu) * lax.rsqrt(var + EPS)
    return (y * g.astype(jnp.float32) + b.astype(jnp.float32)).astype(x.dtype)


def _swiglu(h, w1, w3, w2):
    return (jax.nn.silu(h @ w1) * (h @ w3)) @ w2


def _rotary(x, pos):
    half = x.shape[-1] // 2
    inv_freq = ROPE_BASE ** (-jnp.arange(half, dtype=jnp.float32) / half)
    ang = pos[:, None] * inv_freq[None, :]
    cos = jnp.cos(ang)[None, :, None, :]
    sin = jnp.sin(ang)[None, :, None, :]
    x1, x2 = x[..., :half], x[..., half:]
    return jnp.concatenate([x1 * cos - x2 * sin, x2 * cos + x1 * sin], axis=-1)


def _log_gamma():
    h = jnp.arange(N_RET_HEADS, dtype=jnp.float32)
    return jnp.log1p(-jnp.exp2(-5.0 - h))


def _retention(q, k, v, state0):
    B, T, H, _ = q.shape
    C = T if T <= RET_CHUNK else RET_CHUNK
    n = T // C
    lg = _log_gamma()
    idx = jnp.arange(C, dtype=jnp.float32)
    diff = idx[:, None] - idx[None, :]
    causal = diff >= 0
    dmask = jnp.where(causal[None], jnp.exp(jnp.where(causal, diff, 0.0)[None] * lg[:, None, None]), 0.0)
    q_decay = jnp.exp((idx[:, None] + 1.0) * lg[None, :])
    k_decay = jnp.exp((C - 1.0 - idx[:, None]) * lg[None, :])
    chunk_decay = jnp.exp(C * lg)

    def to_chunks(t):
        return t.reshape(B, n, C, H, t.shape[-1]).transpose(1, 0, 2, 3, 4)

    def step(state, inp):
        qc, kc, vc = inp
        s = jnp.einsum('bihd,bjhd->bhij', qc, kc) * dmask[None]
        inner = jnp.einsum('bhij,bjhv->bihv', s, vc)
        cross = jnp.einsum('bihd,bhdv->bihv', qc * q_decay[None, :, :, None], state)
        new_state = state * chunk_decay[None, :, None, None] + jnp.einsum(
            'bjhd,bjhv->bhdv', kc * k_decay[None, :, :, None], vc)
        return new_state, inner + cross

    state, o = lax.scan(step, state0, (to_chunks(q), to_chunks(k), to_chunks(v)))
    o = o.transpose(1, 0, 2, 3, 4).reshape(B, T, H, v.shape[-1])
    return o, state


def _causal_dwconv(u, buf, w, b):
    full = jnp.concatenate([buf.astype(u.dtype), u], axis=1)
    out = lax.conv_general_dilated(
        full.astype(jnp.float32), w.astype(jnp.float32)[:, None, :],
        window_strides=(1,), padding='VALID',
        dimension_numbers=('NWC', 'WIO', 'NWC'), feature_group_count=u.shape[-1])
    out = (out + b.astype(jnp.float32)).astype(u.dtype)
    return out, full[:, -(CONV_WIDTH - 1):]


def _mixer(h, pos, ret_state, conv_buf, w_in, ret_gn_g, ret_gn_b, conv_w, conv_b,
           conv_ln_g, conv_ln_b, w_out):
    B, T, _ = h.shape
    proj = h @ w_in
    dq = N_RET_HEADS * RET_DK
    q, k, v, g, a, gb = jnp.split(
        proj, np.cumsum([dq, dq, D_RET, D_RET, D_CONV])[:].tolist(), axis=-1)
    q = q.reshape(B, T, N_RET_HEADS, RET_DK).astype(jnp.float32)
    k = k.reshape(B, T, N_RET_HEADS, RET_DK).astype(jnp.float32)
    v = v.reshape(B, T, N_RET_HEADS, RET_DV).astype(jnp.float32)
    q = _rotary(q, pos) * (RET_DK ** -0.5)
    k = _rotary(k, pos)
    o, new_ret = _retention(q, k, v, ret_state.astype(jnp.float32))
    mu = jnp.mean(o, axis=-1, keepdims=True)
    var = jnp.mean(jnp.square(o - mu), axis=-1, keepdims=True)
    o = ((o - mu) * lax.rsqrt(var + EPS)).reshape(B, T, D_RET)
    o = o * ret_gn_g.astype(jnp.float32) + ret_gn_b.astype(jnp.float32)
    ret_y = (jax.nn.silu(g.astype(jnp.float32)) * o).astype(h.dtype)
    u = a * jax.nn.sigmoid(gb)
    c, new_buf = _causal_dwconv(u, conv_buf, conv_w, conv_b)
    conv_y = jax.nn.silu(_layernorm(c, conv_ln_g, conv_ln_b))
    y = jnp.concatenate([ret_y, conv_y], axis=-1) @ w_out
    return y, new_ret.astype(ret_state.dtype), new_buf


def _layer(x, pos, ret_state, conv_buf, l, norm_ffn1_g, ffn1_w1, ffn1_w3, ffn1_w2,
           norm_mix_g, w_in, ret_gn_g, ret_gn_b, conv_w, conv_b, conv_ln_g, conv_ln_b,
           w_out, norm_ffn2_g, ffn2_w1, ffn2_w3, ffn2_w2):
    x = x + 0.5 * _swiglu(_rmsnorm(x, norm_ffn1_g[l]), ffn1_w1[l], ffn1_w3[l], ffn1_w2[l])
    m, new_ret, new_buf = _mixer(_rmsnorm(x, norm_mix_g[l]), pos, ret_state, conv_buf,
                                 w_in[l], ret_gn_g[l], ret_gn_b[l], conv_w[l], conv_b[l],
                                 conv_ln_g[l], conv_ln_b[l], w_out[l])
    x = x + m
    x = x + 0.5 * _swiglu(_rmsnorm(x, norm_ffn2_g[l]), ffn2_w1[l], ffn2_w3[l], ffn2_w2[l])
    return x, new_ret, new_buf


def setup_inputs(seed: int = 0) -> dict:
    key = jax.random.key(seed)
    ks = jax.random.split(key, 24)
    f32 = jnp.float32

    def nrm(k, shape, scale):
        return jax.random.normal(k, shape, f32) * scale

    def gain(k, shape):
        return 1.0 + 0.02 * jax.random.normal(k, shape, f32)

    L = DEPTH
    return {
        "x_prompt": nrm(ks[0], (BATCH, SEQ, D_MODEL), 1.0),
        "x_sample": nrm(ks[1], (DEC_BATCH, DEC_SEQ, D_MODEL), 1.0),
        "state_retention": nrm(ks[2], (L, DEC_BATCH, N_RET_HEADS, RET_DK, RET_DV), 1.0),
        "state_conv": nrm(ks[3], (L, DEC_BATCH, CONV_WIDTH - 1, D_CONV), 0.5),
        "norm_ffn1_g": gain(ks[4], (L, D_MODEL)),
        "ffn1_w1": nrm(ks[5], (L, D_MODEL, D_FF), D_MODEL ** -0.5),
        "ffn1_w3": nrm(ks[6], (L, D_MODEL, D_FF), D_MODEL ** -0.5),
        "ffn1_w2": nrm(ks[7], (L, D_FF, D_MODEL), D_FF ** -0.5),
        "norm_mix_g": gain(ks[8], (L, D_MODEL)),
        "w_in": nrm(ks[9], (L, D_MODEL, D_IN), D_MODEL ** -0.5),
        "ret_gn_g": gain(ks[10], (L, D_RET)),
        "ret_gn_b": nrm(ks[11], (L, D_RET), 0.02),
        "conv_w": nrm(ks[12], (L, CONV_WIDTH, D_CONV), CONV_WIDTH ** -0.5),
        "conv_b": nrm(ks[13], (L, D_CONV), 0.02),
        "conv_ln_g": gain(ks[14], (L, D_CONV)),
        "conv_ln_b": nrm(ks[15], (L, D_CONV), 0.02),
        "w_out": nrm(ks[16], (L, D_MIX, D_MODEL), D_MIX ** -0.5),
        "norm_ffn2_g": gain(ks[17], (L, D_MODEL)),
        "ffn2_w1": nrm(ks[18], (L, D_MODEL, D_FF), D_MODEL ** -0.5),
        "ffn2_w3": nrm(ks[19], (L, D_MODEL, D_FF), D_MODEL ** -0.5),
        "ffn2_w2": nrm(ks[20], (L, D_FF, D_MODEL), D_FF ** -0.5),
        "norm_final_g": gain(ks[21], (D_MODEL,)),
    }


def reference(x_prompt, x_sample, state_retention, state_conv, norm_ffn1_g, ffn1_w1,
              ffn1_w3, ffn1_w2, norm_mix_g, w_in, ret_gn_g, ret_gn_b, conv_w, conv_b,
              conv_ln_g, conv_ln_b, w_out, norm_ffn2_g, ffn2_w1, ffn2_w3, ffn2_w2,
              norm_final_g):
    T_p = x_prompt.shape[1]
    T_s = x_sample.shape[1]
    pos_p = jnp.arange(T_p, dtype=jnp.float32)
    pos_s = PAST_LEN + jnp.arange(T_s, dtype=jnp.float32)
    weights = (norm_ffn1_g, ffn1_w1, ffn1_w3, ffn1_w2, norm_mix_g, w_in, ret_gn_g,
               ret_gn_b, conv_w, conv_b, conv_ln_g, conv_ln_b, w_out, norm_ffn2_g,
               ffn2_w1, ffn2_w3, ffn2_w2)
    hp, hs = x_prompt, x_sample
    ret_p, conv_p, ret_s, conv_s = [], [], [], []
    for l in range(DEPTH):
        r0 = jnp.zeros((x_prompt.shape[0], N_RET_HEADS, RET_DK, RET_DV), state_retention.dtype)
        c0 = jnp.zeros((x_prompt.shape[0], CONV_WIDTH - 1, D_CONV), x_prompt.dtype)
        hp, rp, cp = _layer(hp, pos_p, r0, c0, l, *weights)
        hs, rs, cs = _layer(hs, pos_s, state_retention[l], state_conv[l], l, *weights)
        ret_p.append(rp); conv_p.append(cp); ret_s.append(rs); conv_s.append(cs)
    y_prompt = _rmsnorm(hp, norm_final_g)
    y_sample = _rmsnorm(hs, norm_final_g)
    return (y_prompt, y_sample, jnp.stack(ret_p), jnp.stack(conv_p),
            jnp.stack(ret_s), jnp.stack(conv_s))
```

```python
import functools

import jax
import jax.numpy as jnp
from jax import lax
from jax.experimental import pallas as pl
from jax.experimental.pallas import tpu as pltpu

F32 = jnp.float32
BF16 = jnp.bfloat16

RET_DK = 256
RET_DV = 256
RET_CHUNK = 128
CONV_WIDTH = 31
ROPE_BASE = 10000.0
EPS = 1e-6
PAST_LEN = 16384

LANES = 128
SUBLANES = 8
VMEM_LIMIT_BYTES = 56 * 1024 * 1024

ROPE_HALF = RET_DK // 2
CONV_HIST = CONV_WIDTH - 1
CONV_PAD = 32


def _rms(x, g):
    return x * lax.rsqrt(jnp.mean(x * x, axis=-1, keepdims=True) + EPS) * g


def _params(*semantics):
    return pltpu.CompilerParams(dimension_semantics=semantics,
                                vmem_limit_bytes=VMEM_LIMIT_BYTES)


def _resident(shape):
    nd = len(shape)
    return pl.BlockSpec(shape, lambda *_: (0,) * nd, pipeline_mode=pl.Buffered(1))


def _ffn_body(x_ref, g_ref, w1_ref, w3_ref, w2_ref, gf_ref, o_ref, h_ref, *, final_norm):
    j = pl.program_id(1)

    @pl.when(j == 0)
    def _():
        h_ref[...] = _rms(x_ref[...], g_ref[...]).astype(BF16)
        o_ref[...] = jnp.zeros_like(o_ref)

    h = h_ref[...]
    a = jnp.dot(h, w1_ref[...], preferred_element_type=F32)
    b = jnp.dot(h, w3_ref[...], preferred_element_type=F32)
    act = (jax.nn.silu(a) * b).astype(BF16)
    o_ref[...] += jnp.dot(act, w2_ref[...], preferred_element_type=F32)

    @pl.when(j == pl.num_programs(1) - 1)
    def _():
        y = x_ref[...] + 0.5 * o_ref[...]
        if final_norm:
            y = _rms(y, gf_ref[...])
        o_ref[...] = y


def _ffn(x, g, w1, w3, w2, gf, *, final_norm, tm, tf):
    n, d = x.shape
    f = w1.shape[1]
    return pl.pallas_call(
        functools.partial(_ffn_body, final_norm=final_norm),
        grid=(n // tm, f // tf),
        in_specs=[
            pl.BlockSpec((tm, d), lambda i, j: (i, 0)),
            pl.BlockSpec((1, d), lambda i, j: (0, 0)),
            pl.BlockSpec((d, tf), lambda i, j: (0, j)),
            pl.BlockSpec((d, tf), lambda i, j: (0, j)),
            pl.BlockSpec((tf, d), lambda i, j: (j, 0)),
            pl.BlockSpec((1, d), lambda i, j: (0, 0)),
        ],
        out_specs=pl.BlockSpec((tm, d), lambda i, j: (i, 0)),
        out_shape=jax.ShapeDtypeStruct((n, d), F32),
        scratch_shapes=[pltpu.VMEM((tm, d), BF16)],
        compiler_params=_params("parallel", "arbitrary"),
        name="ffn",
    )(x, g, w1, w3, w2, gf)


def _inproj_body(x_ref, g_ref, w_ref, cos_ref, sin_ref,
                 q_ref, k_ref, v_ref, gate_ref, u_ref, *, n_heads, d_ret, d_conv):
    h = _rms(x_ref[...], g_ref[...]).astype(BF16)
    cos = cos_ref[...]
    sin = sin_ref[...]

    def proj(col, width):
        return jnp.dot(h, w_ref[:, col:col + width], preferred_element_type=F32)

    def rotary(t):
        t1, t2 = t[:, :ROPE_HALF], t[:, ROPE_HALF:]
        return jnp.concatenate([t1 * cos - t2 * sin, t2 * cos + t1 * sin], axis=-1)

    dq = n_heads * RET_DK
    for hd in range(n_heads):
        lo = hd * RET_DK
        q_ref[:, lo:lo + RET_DK] = (rotary(proj(lo, RET_DK)) * (RET_DK ** -0.5)).astype(BF16)
        k_ref[:, lo:lo + RET_DK] = rotary(proj(dq + lo, RET_DK)).astype(BF16)
    v_ref[...] = proj(2 * dq, d_ret).astype(BF16)
    gate_ref[...] = proj(2 * dq + d_ret, d_ret).astype(BF16)
    a = proj(2 * dq + 2 * d_ret, d_conv)
    b = proj(2 * dq + 2 * d_ret + d_conv, d_conv)
    u_ref[...] = a * jax.nn.sigmoid(b)


def _inproj(x, g, w_in, cos, sin, *, n_heads, d_ret, d_conv, tm):
    n, d = x.shape
    period = cos.shape[0] // tm
    dq = n_heads * RET_DK
    row = lambda i: (i, 0)
    return pl.pallas_call(
        functools.partial(_inproj_body, n_heads=n_heads, d_ret=d_ret, d_conv=d_conv),
        grid=(n // tm,),
        in_specs=[
            pl.BlockSpec((tm, d), row),
            _resident((1, d)),
            _resident(w_in.shape),
            pl.BlockSpec((tm, ROPE_HALF), lambda i: (i % period, 0)),
            pl.BlockSpec((tm, ROPE_HALF), lambda i: (i % period, 0)),
        ],
        out_specs=[
            pl.BlockSpec((tm, dq), row),
            pl.BlockSpec((tm, dq), row),
            pl.BlockSpec((tm, d_ret), row),
            pl.BlockSpec((tm, d_ret), row),
            pl.BlockSpec((tm, d_conv), row),
        ],
        out_shape=[
            jax.ShapeDtypeStruct((n, dq), BF16),
            jax.ShapeDtypeStruct((n, dq), BF16),
            jax.ShapeDtypeStruct((n, d_ret), BF16),
            jax.ShapeDtypeStruct((n, d_ret), BF16),
            jax.ShapeDtypeStruct((n, d_conv), F32),
        ],
        compiler_params=_params("parallel"),
        name="inproj",
    )(x, g, w_in, cos, sin)


def _retention_body(q_ref, k_ref, v_ref, gate_ref, s0_ref, dmask_ref, qdec_ref, kdec_ref,
                    cdec_ref, gng_ref, gnb_ref, y_ref, so_ref, *, n_heads, nb, nc, c):
    @pl.when(pl.program_id(1) == 0)
    def _():
        so_ref[...] = s0_ref[...]

    narrow = c % (2 * SUBLANES) != 0
    mm = F32 if narrow else BF16
    nt = (((1,), (1,)), ((), ()))
    tn = (((0,), (0,)), ((), ()))

    for hd in range(n_heads):
        cols = slice(hd * RET_DK, (hd + 1) * RET_DK)
        vcols = slice(hd * RET_DV, (hd + 1) * RET_DV)
        dmask = dmask_ref[hd]
        qdec = qdec_ref[hd]
        kdec = kdec_ref[hd]
        cdec = cdec_ref[hd]
        gng = gng_ref[:, vcols]
        gnb = gnb_ref[:, vcols]
        if narrow:
            q_all = q_ref[:, cols].astype(F32)
            k_all = k_ref[:, cols].astype(F32)
            v_all = v_ref[:, vcols].astype(F32)
            gate_all = gate_ref[:, vcols].astype(F32)
        for bb in range(nb):
            state = so_ref[bb, hd]
            for cc in range(nc):
                rows = slice((bb * nc + cc) * c, (bb * nc + cc + 1) * c)
                if narrow:
                    q, k, v, gate = q_all[rows], k_all[rows], v_all[rows], gate_all[rows]
                else:
                    q, k, v = q_ref[rows, cols], k_ref[rows, cols], v_ref[rows, vcols]
                    gate = gate_ref[rows, vcols].astype(F32)
                s = lax.dot_general(q, k, nt, preferred_element_type=F32) * dmask
                inner = jnp.dot(s.astype(mm), v, preferred_element_type=F32)
                qd = (q.astype(F32) * qdec).astype(mm)
                cross = jnp.dot(qd, state.astype(mm), preferred_element_type=F32)
                o = inner + cross
                kd = (k.astype(F32) * kdec).astype(mm)
                state = state * cdec + lax.dot_general(kd, v, tn, preferred_element_type=F32)
                mu = jnp.mean(o, axis=-1, keepdims=True)
                var = jnp.mean(jnp.square(o - mu), axis=-1, keepdims=True)
                on = (o - mu) * lax.rsqrt(var + EPS) * gng + gnb
                y_ref[rows, vcols] = (jax.nn.silu(gate) * on).astype(y_ref.dtype)
            so_ref[bb, hd] = state


def _retention(q, k, v, gate, state0, gng, gnb, *, seq, c, nb, nc):
    n, d_ret = v.shape
    batch, n_heads = state0.shape[:2]
    steps = seq // (nc * c)
    assert nb == 1 or steps == 1

    lg = jnp.log1p(-jnp.exp2(-5.0 - jnp.arange(n_heads, dtype=F32)))
    idx = jnp.arange(c, dtype=F32)
    diff = idx[:, None] - idx[None, :]
    causal = diff >= 0
    dmask = jnp.where(causal[None], jnp.exp(jnp.where(causal, diff, 0.0)[None] * lg[:, None, None]), 0.0)
    qdec = jnp.broadcast_to(jnp.exp((idx[None, :] + 1.0) * lg[:, None])[:, :, None], (n_heads, c, RET_DK))
    kdec = jnp.broadcast_to(jnp.exp((c - 1.0 - idx[None, :]) * lg[:, None])[:, :, None], (n_heads, c, RET_DK))
    cdec = jnp.broadcast_to(jnp.exp(c * lg)[:, None, None], (n_heads, 1, RET_DV))

    rows = nb * nc * c
    row = lambda b, t: (b * steps + t, 0)
    st = lambda b, t: (b, 0, 0, 0)
    y_dtype = BF16 if c % (2 * SUBLANES) == 0 else F32
    return pl.pallas_call(
        functools.partial(_retention_body, n_heads=n_heads, nb=nb, nc=nc, c=c),
        grid=(batch // nb, steps),
        in_specs=[
            pl.BlockSpec((rows, q.shape[1]), row),
            pl.BlockSpec((rows, k.shape[1]), row),
            pl.BlockSpec((rows, d_ret), row),
            pl.BlockSpec((rows, d_ret), row),
            pl.BlockSpec((nb,) + state0.shape[1:], st),
            _resident(dmask.shape),
            _resident(qdec.shape),
            _resident(kdec.shape),
            _resident(cdec.shape),
            _resident(gng.shape),
            _resident(gnb.shape),
        ],
        out_specs=[
            pl.BlockSpec((rows, d_ret), row),
            pl.BlockSpec((nb,) + state0.shape[1:], st),
        ],
        out_shape=[
            jax.ShapeDtypeStruct((n, d_ret), y_dtype),
            jax.ShapeDtypeStruct(state0.shape, state0.dtype),
        ],
        compiler_params=_params("parallel", "arbitrary"),
        name="retention",
    )(q, k, v, gate, state0, dmask, qdec, kdec, cdec, gng, gnb)


def _conv_body(u_ref, buf_ref, w_ref, b_ref, lng_ref, lnb_ref, y_ref, nbuf_ref, full_ref,
               *, bb, seq, rt):
    d = u_ref.shape[-1]
    head = CONV_PAD - CONV_HIST
    span = rt + CONV_PAD
    bias, lng, lnb = b_ref[...], lng_ref[...], lnb_ref[...]

    def row_tile(t0, b):
        parts = []
        for lc in range(d // LANES):
            lanes = slice(lc * LANES, (lc + 1) * LANES)
            win = full_ref[pl.ds(t0, span), lanes]
            acc = jnp.zeros((rt, LANES), F32)
            for s in range(SUBLANES):
                shifted = win if s == 0 else win[s:s + span - SUBLANES]
                for a in range((shifted.shape[0] - rt) // SUBLANES + 1):
                    j = a * SUBLANES + s - head
                    if 0 <= j < CONV_WIDTH:
                        acc = acc + shifted[a * SUBLANES:a * SUBLANES + rt] * w_ref[j:j + 1, lanes]
            parts.append(acc)
        cv = jnp.concatenate(parts, axis=-1) + bias
        mu = jnp.mean(cv, axis=-1, keepdims=True)
        var = jnp.mean(jnp.square(cv - mu), axis=-1, keepdims=True)
        ln = (cv - mu) * lax.rsqrt(var + EPS) * lng + lnb
        y_ref[b, pl.ds(t0, rt), :] = jax.nn.silu(ln).astype(y_ref.dtype)

    for b in range(bb):
        full_ref[0:SUBLANES, :] = jnp.zeros((SUBLANES, d), F32)
        full_ref[head:CONV_PAD, :] = buf_ref[b]
        full_ref[CONV_PAD:CONV_PAD + seq, :] = u_ref[b]
        if seq == rt:
            row_tile(0, b)
        else:
            def step(i, carry):
                row_tile(pl.multiple_of(i * rt, rt), b)
                return carry
            lax.fori_loop(0, seq // rt, step, 0)
        nbuf_ref[b] = full_ref[seq + head:seq + CONV_PAD, :]


def _conv(u, buf, w, b, lng, lnb, *, bb, rt):
    batch, seq, d = u.shape
    y_dtype = BF16 if seq % (2 * SUBLANES) == 0 else F32
    blk = lambda i: (i, 0, 0)
    return pl.pallas_call(
        functools.partial(_conv_body, bb=bb, seq=seq, rt=rt),
        grid=(batch // bb,),
        in_specs=[
            pl.BlockSpec((bb, seq, d), blk),
            pl.BlockSpec((bb, CONV_HIST, d), blk),
            _resident(w.shape),
            _resident(b.shape),
            _resident(lng.shape),
            _resident(lnb.shape),
        ],
        out_specs=[
            pl.BlockSpec((bb, seq, d), blk),
            pl.BlockSpec((bb, CONV_HIST, d), blk),
        ],
        out_shape=[
            jax.ShapeDtypeStruct((batch, seq, d), y_dtype),
            jax.ShapeDtypeStruct((batch, CONV_HIST, d), u.dtype),
        ],
        scratch_shapes=[pltpu.VMEM((seq + CONV_PAD, d), F32)],
        compiler_params=_params("parallel"),
        name="conv",
    )(u, buf, w, b, lng, lnb)


def _outproj_body(x_ref, r_ref, c_ref, w_ref, o_ref, *, d_ret):
    y = jnp.dot(r_ref[...].astype(BF16), w_ref[:d_ret, :], preferred_element_type=F32)
    y += jnp.dot(c_ref[...].astype(BF16), w_ref[d_ret:, :], preferred_element_type=F32)
    o_ref[...] = x_ref[...] + y


def _outproj(x, ret_y, conv_y, w_out, *, tm):
    n, d = x.shape
    d_ret, d_conv = ret_y.shape[1], conv_y.shape[1]
    row = lambda i: (i, 0)
    return pl.pallas_call(
        functools.partial(_outproj_body, d_ret=d_ret),
        grid=(n // tm,),
        in_specs=[
            pl.BlockSpec((tm, d), row),
            pl.BlockSpec((tm, d_ret), row),
            pl.BlockSpec((tm, d_conv), row),
            _resident(w_out.shape),
        ],
        out_specs=pl.BlockSpec((tm, d), row),
        out_shape=jax.ShapeDtypeStruct((n, d), F32),
        compiler_params=_params("parallel"),
        name="outproj",
    )(x, ret_y, conv_y, w_out)


def _rope_tables(pos, rows):
    inv_freq = ROPE_BASE ** (-jnp.arange(ROPE_HALF, dtype=F32) / ROPE_HALF)
    ang = pos[:, None] * inv_freq[None, :]
    reps = max(1, rows // pos.shape[0])
    return jnp.tile(jnp.cos(ang), (reps, 1)), jnp.tile(jnp.sin(ang), (reps, 1))


def _stream(x, pos, ret_state, conv_buf, wts, *, tm, tf, ret_nb, ret_nc, conv_bb, conv_rt):
    batch, seq, d = x.shape
    n_heads = ret_state.shape[1]
    d_ret = n_heads * RET_DV
    d_conv = conv_buf.shape[-1]
    c = min(seq, RET_CHUNK)
    xf = x.reshape(batch * seq, d)

    x1 = _ffn(xf, wts["norm_ffn1_g"], wts["ffn1_w1"], wts["ffn1_w3"], wts["ffn1_w2"],
              wts["norm_final_g"], final_norm=False, tm=tm, tf=tf)
    cos, sin = _rope_tables(pos, tm)
    q, k, v, gate, u = _inproj(x1, wts["norm_mix_g"], wts["w_in"], cos, sin,
                               n_heads=n_heads, d_ret=d_ret, d_conv=d_conv, tm=tm)
    ret_y, new_ret = _retention(q, k, v, gate, ret_state, wts["ret_gn_g"], wts["ret_gn_b"],
                                seq=seq, c=c, nb=ret_nb, nc=ret_nc)
    conv_y, new_buf = _conv(u.reshape(batch, seq, d_conv), conv_buf, wts["conv_w"], wts["conv_b"],
                            wts["conv_ln_g"], wts["conv_ln_b"], bb=conv_bb, rt=conv_rt)
    x2 = _outproj(x1, ret_y, conv_y.reshape(batch * seq, d_conv), wts["w_out"], tm=tm)
    y = _ffn(x2, wts["norm_ffn2_g"], wts["ffn2_w1"], wts["ffn2_w3"], wts["ffn2_w2"],
             wts["norm_final_g"], final_norm=True, tm=tm, tf=tf)
    return y.reshape(batch, seq, d), new_ret, new_buf


def kernel(x_prompt, x_sample, state_retention, state_conv, norm_ffn1_g, ffn1_w1, ffn1_w3, ffn1_w2, norm_mix_g, w_in, ret_gn_g, ret_gn_b, conv_w, conv_b, conv_ln_g, conv_ln_b, w_out, norm_ffn2_g, ffn2_w1, ffn2_w3, ffn2_w2, norm_final_g):
    depth = w_in.shape[0]
    assert depth == 1
    layer = 0
    row = lambda g: g.reshape(1, -1)
    wts = {
        "norm_ffn1_g": row(norm_ffn1_g[layer]),
        "ffn1_w1": ffn1_w1[layer].astype(BF16),
        "ffn1_w3": ffn1_w3[layer].astype(BF16),
        "ffn1_w2": ffn1_w2[layer].astype(BF16),
        "norm_mix_g": row(norm_mix_g[layer]),
        "w_in": w_in[layer].astype(BF16),
        "ret_gn_g": row(ret_gn_g[layer]),
        "ret_gn_b": row(ret_gn_b[layer]),
        "conv_w": conv_w[layer],
        "conv_b": row(conv_b[layer]),
        "conv_ln_g": row(conv_ln_g[layer]),
        "conv_ln_b": row(conv_ln_b[layer]),
        "w_out": w_out[layer].astype(BF16),
        "norm_ffn2_g": row(norm_ffn2_g[layer]),
        "ffn2_w1": ffn2_w1[layer].astype(BF16),
        "ffn2_w3": ffn2_w3[layer].astype(BF16),
        "ffn2_w2": ffn2_w2[layer].astype(BF16),
        "norm_final_g": row(norm_final_g),
    }
    bp, tp, _ = x_prompt.shape
    bs, ts, _ = x_sample.shape
    n_heads = state_retention.shape[2]
    d_conv = state_conv.shape[-1]

    pos_p = jnp.arange(tp, dtype=F32)
    pos_s = PAST_LEN + jnp.arange(ts, dtype=F32)
    r0 = jnp.zeros((bp, n_heads, RET_DK, RET_DV), state_retention.dtype)
    c0 = jnp.zeros((bp, CONV_HIST, d_conv), x_prompt.dtype)

    y_p, ret_p, conv_p = _stream(x_prompt, pos_p, r0, c0, wts, tm=512, tf=512,
                                 ret_nb=1, ret_nc=4, conv_bb=1, conv_rt=64)
    y_s, ret_s, conv_s = _stream(x_sample, pos_s, state_retention[layer], state_conv[layer], wts,
                                 tm=512, tf=512, ret_nb=4, ret_nc=1, conv_bb=16, conv_rt=ts)
    return (y_p, y_s, ret_p[None], conv_p[None], ret_s[None], conv_s[None])
```

```python
import functools

import jax
import jax.numpy as jnp
from jax import lax
from jax.experimental import pallas as pl
from jax.experimental.pallas import tpu as pltpu

F32 = jnp.float32
BF16 = jnp.bfloat16

RET_DK = 256
RET_DV = 256
RET_CHUNK = 128
CONV_WIDTH = 31
ROPE_BASE = 10000.0
EPS = 1e-6
PAST_LEN = 16384

LANES = 128
SUBLANES = 8
VMEM_LIMIT_BYTES = 56 * 1024 * 1024

ROPE_HALF = RET_DK // 2
CONV_HIST = CONV_WIDTH - 1
CONV_PAD = 32


def _rms(x, g):
    return x * lax.rsqrt(jnp.mean(x * x, axis=-1, keepdims=True) + EPS) * g


def _params(*semantics):
    return pltpu.CompilerParams(dimension_semantics=semantics,
                                vmem_limit_bytes=VMEM_LIMIT_BYTES)


def _resident(shape):
    nd = len(shape)
    return pl.BlockSpec(shape, lambda *_: (0,) * nd, pipeline_mode=pl.Buffered(1))


def _ffn_body(x_ref, g_ref, w1_ref, w3_ref, w2_ref, gf_ref, o_ref, h_ref, *, final_norm):
    j = pl.program_id(1)

    @pl.when(j == 0)
    def _():
        h_ref[...] = _rms(x_ref[...], g_ref[...]).astype(BF16)
        o_ref[...] = jnp.zeros_like(o_ref)

    h = h_ref[...]
    a = jnp.dot(h, w1_ref[...].astype(BF16), preferred_element_type=F32)
    b = jnp.dot(h, w3_ref[...].astype(BF16), preferred_element_type=F32)
    act = (jax.nn.silu(a) * b).astype(BF16)
    o_ref[...] += jnp.dot(act, w2_ref[...].astype(BF16), preferred_element_type=F32)

    @pl.when(j == pl.num_programs(1) - 1)
    def _():
        y = x_ref[...] + 0.5 * o_ref[...]
        if final_norm:
            y = _rms(y, gf_ref[...])
        o_ref[...] = y


def _ffn(x, g, w1, w3, w2, gf, *, final_norm, tm, tf):
    n, d = x.shape
    f = w1.shape[1]
    return pl.pallas_call(
        functools.partial(_ffn_body, final_norm=final_norm),
        grid=(n // tm, f // tf),
        in_specs=[
            pl.BlockSpec((tm, d), lambda i, j: (i, 0), pipeline_mode=pl.Buffered(1)),
            pl.BlockSpec((1, d), lambda i, j: (0, 0)),
            pl.BlockSpec((d, tf), lambda i, j: (0, j)),
            pl.BlockSpec((d, tf), lambda i, j: (0, j)),
            pl.BlockSpec((tf, d), lambda i, j: (j, 0)),
            pl.BlockSpec((1, d), lambda i, j: (0, 0)),
        ],
        out_specs=pl.BlockSpec((tm, d), lambda i, j: (i, 0)),
        out_shape=jax.ShapeDtypeStruct((n, d), F32),
        scratch_shapes=[pltpu.VMEM((tm, d), BF16)],
        compiler_params=_params("parallel", "arbitrary"),
        name="ffn",
    )(x, g, w1, w3, w2, gf)


def _inproj_body(x_ref, g_ref, w_ref, cos_ref, sin_ref,
                 q_ref, k_ref, v_ref, gate_ref, u_ref, *, n_heads, d_ret, d_conv):
    h = _rms(x_ref[...], g_ref[...]).astype(BF16)
    cos = cos_ref[...]
    sin = sin_ref[...]

    def proj(col, width):
        return jnp.dot(h, w_ref[:, col:col + width], preferred_element_type=F32)

    def rotary(t):
        t1, t2 = t[:, :ROPE_HALF], t[:, ROPE_HALF:]
        return jnp.concatenate([t1 * cos - t2 * sin, t2 * cos + t1 * sin], axis=-1)

    dq = n_heads * RET_DK
    for hd in range(n_heads):
        lo = hd * RET_DK
        q_ref[:, lo:lo + RET_DK] = (rotary(proj(lo, RET_DK)) * (RET_DK ** -0.5)).astype(BF16)
        k_ref[:, lo:lo + RET_DK] = rotary(proj(dq + lo, RET_DK)).astype(BF16)
    v_ref[...] = proj(2 * dq, d_ret).astype(BF16)
    gate_ref[...] = proj(2 * dq + d_ret, d_ret).astype(BF16)
    a = proj(2 * dq + 2 * d_ret, d_conv)
    b = proj(2 * dq + 2 * d_ret + d_conv, d_conv)
    u_ref[...] = a * jax.nn.sigmoid(b)


def _inproj(x, g, w_in, cos, sin, *, n_heads, d_ret, d_conv, tm):
    n, d = x.shape
    period = cos.shape[0] // tm
    dq = n_heads * RET_DK
    row = lambda i: (i, 0)
    return pl.pallas_call(
        functools.partial(_inproj_body, n_heads=n_heads, d_ret=d_ret, d_conv=d_conv),
        grid=(n // tm,),
        in_specs=[
            pl.BlockSpec((tm, d), row),
            _resident((1, d)),
            _resident(w_in.shape),
            pl.BlockSpec((tm, ROPE_HALF), lambda i: (i % period, 0)),
            pl.BlockSpec((tm, ROPE_HALF), lambda i: (i % period, 0)),
        ],
        out_specs=[
            pl.BlockSpec((tm, dq), row),
            pl.BlockSpec((tm, dq), row),
            pl.BlockSpec((tm, d_ret), row),
            pl.BlockSpec((tm, d_ret), row),
            pl.BlockSpec((tm, d_conv), row),
        ],
        out_shape=[
            jax.ShapeDtypeStruct((n, dq), BF16),
            jax.ShapeDtypeStruct((n, dq), BF16),
            jax.ShapeDtypeStruct((n, d_ret), BF16),
            jax.ShapeDtypeStruct((n, d_ret), BF16),
            jax.ShapeDtypeStruct((n, d_conv), F32),
        ],
        compiler_params=_params("parallel"),
        name="inproj",
    )(x, g, w_in, cos, sin)


def _retention_body(q_ref, k_ref, v_ref, gate_ref, s0_ref, dmask_ref, qdec_ref, kdec_ref,
                    cdec_ref, gng_ref, gnb_ref, y_ref, so_ref, *, n_heads, nb, nc, c):
    @pl.when(pl.program_id(1) == 0)
    def _():
        so_ref[...] = s0_ref[...]

    narrow = c % (2 * SUBLANES) != 0
    mm = F32 if narrow else BF16
    nt = (((1,), (1,)), ((), ()))
    tn = (((0,), (0,)), ((), ()))

    for hd in range(n_heads):
        cols = slice(hd * RET_DK, (hd + 1) * RET_DK)
        vcols = slice(hd * RET_DV, (hd + 1) * RET_DV)
        dmask = dmask_ref[hd]
        qdec = qdec_ref[hd]
        kdec = kdec_ref[hd]
        cdec = cdec_ref[hd]
        gng = gng_ref[:, vcols]
        gnb = gnb_ref[:, vcols]
        if narrow:
            q_all = q_ref[:, cols].astype(F32)
            k_all = k_ref[:, cols].astype(F32)
            v_all = v_ref[:, vcols].astype(F32)
            gate_all = gate_ref[:, vcols].astype(F32)
        for bb in range(nb):
            state = so_ref[bb, hd]
            for cc in range(nc):
                rows = slice((bb * nc + cc) * c, (bb * nc + cc + 1) * c)
                if narrow:
                    q, k, v, gate = q_all[rows], k_all[rows], v_all[rows], gate_all[rows]
                else:
                    q, k, v = q_ref[rows, cols], k_ref[rows, cols], v_ref[rows, vcols]
                    gate = gate_ref[rows, vcols].astype(F32)
                s = lax.dot_general(q, k, nt, preferred_element_type=F32) * dmask
                inner = jnp.dot(s.astype(mm), v, preferred_element_type=F32)
                qd = (q.astype(F32) * qdec).astype(mm)
                cross = jnp.dot(qd, state.astype(mm), preferred_element_type=F32)
                o = inner + cross
                kd = (k.astype(F32) * kdec).astype(mm)
                state = state * cdec + lax.dot_general(kd, v, tn, preferred_element_type=F32)
                mu = jnp.mean(o, axis=-1, keepdims=True)
                var = jnp.mean(jnp.square(o - mu), axis=-1, keepdims=True)
                on = (o - mu) * lax.rsqrt(var + EPS) * gng + gnb
                y_ref[rows, vcols] = (jax.nn.silu(gate) * on).astype(y_ref.dtype)
            so_ref[bb, hd] = state


def _retention(q, k, v, gate, state0, gng, gnb, *, seq, c, nb, nc):
    n, d_ret = v.shape
    batch, n_heads = state0.shape[:2]
    steps = seq // (nc * c)
    assert nb == 1 or steps == 1

    lg = jnp.log1p(-jnp.exp2(-5.0 - jnp.arange(n_heads, dtype=F32)))
    idx = jnp.arange(c, dtype=F32)
    diff = idx[:, None] - idx[None, :]
    causal = diff >= 0
    dmask = jnp.where(causal[None], jnp.exp(jnp.where(causal, diff, 0.0)[None] * lg[:, None, None]), 0.0)
    qdec = jnp.broadcast_to(jnp.exp((idx[None, :] + 1.0) * lg[:, None])[:, :, None], (n_heads, c, RET_DK))
    kdec = jnp.broadcast_to(jnp.exp((c - 1.0 - idx[None, :]) * lg[:, None])[:, :, None], (n_heads, c, RET_DK))
    cdec = jnp.broadcast_to(jnp.exp(c * lg)[:, None, None], (n_heads, 1, RET_DV))

    rows = nb * nc * c
    row = lambda b, t: (b * steps + t, 0)
    st = lambda b, t: (b, 0, 0, 0)
    y_dtype = BF16 if c % (2 * SUBLANES) == 0 else F32
    return pl.pallas_call(
        functools.partial(_retention_body, n_heads=n_heads, nb=nb, nc=nc, c=c),
        grid=(batch // nb, steps),
        in_specs=[
            pl.BlockSpec((rows, q.shape[1]), row),
            pl.BlockSpec((rows, k.shape[1]), row),
            pl.BlockSpec((rows, d_ret), row),
            pl.BlockSpec((rows, d_ret), row),
            pl.BlockSpec((nb,) + state0.shape[1:], st),
            _resident(dmask.shape),
            _resident(qdec.shape),
            _resident(kdec.shape),
            _resident(cdec.shape),
            _resident(gng.shape),
            _resident(gnb.shape),
        ],
        out_specs=[
            pl.BlockSpec((rows, d_ret), row),
            pl.BlockSpec((nb,) + state0.shape[1:], st),
        ],
        out_shape=[
            jax.ShapeDtypeStruct((n, d_ret), y_dtype),
            jax.ShapeDtypeStruct(state0.shape, state0.dtype),
        ],
        compiler_params=_params("parallel", "arbitrary"),
        name="retention",
    )(q, k, v, gate, state0, dmask, qdec, kdec, cdec, gng, gnb)


def _conv_body(u_ref, buf_ref, w_ref, b_ref, lng_ref, lnb_ref, y_ref, nbuf_ref, full_ref, cv_ref,
               *, bb, seq, rt):
    d = u_ref.shape[-1]
    n_lc = d // LANES
    groups = rt // SUBLANES
    head = CONV_PAD - CONV_HIST

    def row_tile(t0, b):
        def conv_lanes(lc, carry):
            taps = [jnp.broadcast_to(w_ref[lc, j:j + 1, :], (SUBLANES, LANES))
                    for j in range(CONV_WIDTH)]
            accs = [jnp.zeros((SUBLANES, LANES), F32) for _ in range(groups)]
            for o in range(CONV_WIDTH + rt - SUBLANES):
                xo = full_ref[lc, pl.ds(t0 + (head + o), SUBLANES), :]
                for r in range(groups):
                    j = o - r * SUBLANES
                    if 0 <= j < CONV_WIDTH:
                        accs[r] = accs[r] + xo * taps[j]
            cv_ref[lc] = jnp.concatenate(accs, axis=0)
            return carry

        lax.fori_loop(0, n_lc, conv_lanes, 0)
        cv = jnp.concatenate([cv_ref[lc] for lc in range(n_lc)], axis=-1) + b_ref[...]
        mu = jnp.mean(cv, axis=-1, keepdims=True)
        var = jnp.mean(jnp.square(cv - mu), axis=-1, keepdims=True)
        ln = (cv - mu) * lax.rsqrt(var + EPS) * lng_ref[...] + lnb_ref[...]
        y_ref[b, pl.ds(t0, rt), :] = jax.nn.silu(ln).astype(y_ref.dtype)

    for b in range(bb):
        for lc in range(n_lc):
            lanes = slice(lc * LANES, (lc + 1) * LANES)
            full_ref[lc, 0:SUBLANES, :] = jnp.zeros((SUBLANES, LANES), F32)
            full_ref[lc, head:CONV_PAD, :] = buf_ref[b, :, lanes]
            full_ref[lc, CONV_PAD:CONV_PAD + seq, :] = u_ref[b, :, lanes]
        if seq == rt:
            row_tile(0, b)
        else:
            def step(i, carry):
                row_tile(pl.multiple_of(i * rt, rt), b)
                return carry
            lax.fori_loop(0, seq // rt, step, 0)
        for lc in range(n_lc):
            lanes = slice(lc * LANES, (lc + 1) * LANES)
            nbuf_ref[b, :, lanes] = full_ref[lc, seq + head:seq + CONV_PAD, :]


def _conv(u, buf, w, b, lng, lnb, *, bb, rt):
    batch, seq, d = u.shape
    w = w.reshape(w.shape[0], d // LANES, LANES).transpose(1, 0, 2)
    y_dtype = BF16 if seq % (2 * SUBLANES) == 0 else F32
    blk = lambda i: (i, 0, 0)
    return pl.pallas_call(
        functools.partial(_conv_body, bb=bb, seq=seq, rt=rt),
        grid=(batch // bb,),
        in_specs=[
            pl.BlockSpec((bb, seq, d), blk),
            pl.BlockSpec((bb, CONV_HIST, d), blk),
            _resident(w.shape),
            _resident(b.shape),
            _resident(lng.shape),
            _resident(lnb.shape),
        ],
        out_specs=[
            pl.BlockSpec((bb, seq, d), blk),
            pl.BlockSpec((bb, CONV_HIST, d), blk),
        ],
        out_shape=[
            jax.ShapeDtypeStruct((batch, seq, d), y_dtype),
            jax.ShapeDtypeStruct((batch, CONV_HIST, d), u.dtype),
        ],
        scratch_shapes=[pltpu.VMEM((d // LANES, seq + CONV_PAD, LANES), F32),
                        pltpu.VMEM((d // LANES, rt, LANES), F32)],
        compiler_params=_params("parallel"),
        name="conv",
    )(u, buf, w, b, lng, lnb)


def _outproj_body(x_ref, r_ref, c_ref, w_ref, o_ref, *, d_ret):
    y = jnp.dot(r_ref[...].astype(BF16), w_ref[:d_ret, :], preferred_element_type=F32)
    y += jnp.dot(c_ref[...].astype(BF16), w_ref[d_ret:, :], preferred_element_type=F32)
    o_ref[...] = x_ref[...] + y


def _outproj(x, ret_y, conv_y, w_out, *, tm):
    n, d = x.shape
    d_ret, d_conv = ret_y.shape[1], conv_y.shape[1]
    row = lambda i: (i, 0)
    return pl.pallas_call(
        functools.partial(_outproj_body, d_ret=d_ret),
        grid=(n // tm,),
        in_specs=[
            pl.BlockSpec((tm, d), row),
            pl.BlockSpec((tm, d_ret), row),
            pl.BlockSpec((tm, d_conv), row),
            _resident(w_out.shape),
        ],
        out_specs=pl.BlockSpec((tm, d), row),
        out_shape=jax.ShapeDtypeStruct((n, d), F32),
        compiler_params=_params("parallel"),
        name="outproj",
    )(x, ret_y, conv_y, w_out)


def _rope_tables(pos, rows):
    inv_freq = ROPE_BASE ** (-jnp.arange(ROPE_HALF, dtype=F32) / ROPE_HALF)
    ang = pos[:, None] * inv_freq[None, :]
    reps = max(1, rows // pos.shape[0])
    return jnp.tile(jnp.cos(ang), (reps, 1)), jnp.tile(jnp.sin(ang), (reps, 1))


def _stream(x, pos, ret_state, conv_buf, wts, *, ffn_tm, tf, tm, ret_nb, ret_nc, conv_bb, conv_rt):
    batch, seq, d = x.shape
    n_heads = ret_state.shape[1]
    d_ret = n_heads * RET_DV
    d_conv = conv_buf.shape[-1]
    c = min(seq, RET_CHUNK)
    xf = x.reshape(batch * seq, d)

    x1 = _ffn(xf, wts["norm_ffn1_g"], wts["ffn1_w1"], wts["ffn1_w3"], wts["ffn1_w2"],
              wts["norm_final_g"], final_norm=False, tm=ffn_tm, tf=tf)
    cos, sin = _rope_tables(pos, tm)
    q, k, v, gate, u = _inproj(x1, wts["norm_mix_g"], wts["w_in"], cos, sin,
                               n_heads=n_heads, d_ret=d_ret, d_conv=d_conv, tm=tm)
    ret_y, new_ret = _retention(q, k, v, gate, ret_state, wts["ret_gn_g"], wts["ret_gn_b"],
                                seq=seq, c=c, nb=ret_nb, nc=ret_nc)
    conv_y, new_buf = _conv(u.reshape(batch, seq, d_conv), conv_buf, wts["conv_w"], wts["conv_b"],
                            wts["conv_ln_g"], wts["conv_ln_b"], bb=conv_bb, rt=conv_rt)
    x2 = _outproj(x1, ret_y, conv_y.reshape(batch * seq, d_conv), wts["w_out"], tm=tm)
    y = _ffn(x2, wts["norm_ffn2_g"], wts["ffn2_w1"], wts["ffn2_w3"], wts["ffn2_w2"],
             wts["norm_final_g"], final_norm=True, tm=ffn_tm, tf=tf)
    return y.reshape(batch, seq, d), new_ret, new_buf


def kernel(x_prompt, x_sample, state_retention, state_conv, norm_ffn1_g, ffn1_w1, ffn1_w3, ffn1_w2, norm_mix_g, w_in, ret_gn_g, ret_gn_b, conv_w, conv_b, conv_ln_g, conv_ln_b, w_out, norm_ffn2_g, ffn2_w1, ffn2_w3, ffn2_w2, norm_final_g):
    depth = w_in.shape[0]
    assert depth == 1
    layer = 0
    row = lambda g: g.reshape(1, -1)
    wts = {
        "norm_ffn1_g": row(norm_ffn1_g[layer]),
        "ffn1_w1": ffn1_w1[layer],
        "ffn1_w3": ffn1_w3[layer],
        "ffn1_w2": ffn1_w2[layer],
        "norm_mix_g": row(norm_mix_g[layer]),
        "w_in": w_in[layer].astype(BF16),
        "ret_gn_g": row(ret_gn_g[layer]),
        "ret_gn_b": row(ret_gn_b[layer]),
        "conv_w": conv_w[layer],
        "conv_b": row(conv_b[layer]),
        "conv_ln_g": row(conv_ln_g[layer]),
        "conv_ln_b": row(conv_ln_b[layer]),
        "w_out": w_out[layer].astype(BF16),
        "norm_ffn2_g": row(norm_ffn2_g[layer]),
        "ffn2_w1": ffn2_w1[layer],
        "ffn2_w3": ffn2_w3[layer],
        "ffn2_w2": ffn2_w2[layer],
        "norm_final_g": row(norm_final_g),
    }
    bp, tp, _ = x_prompt.shape
    bs, ts, _ = x_sample.shape
    n_heads = state_retention.shape[2]
    d_conv = state_conv.shape[-1]

    pos_p = jnp.arange(tp, dtype=F32)
    pos_s = PAST_LEN + jnp.arange(ts, dtype=F32)
    r0 = jnp.zeros((bp, n_heads, RET_DK, RET_DV), state_retention.dtype)
    c0 = jnp.zeros((bp, CONV_HIST, d_conv), x_prompt.dtype)

    y_p, ret_p, conv_p = _stream(x_prompt, pos_p, r0, c0, wts, ffn_tm=1024, tf=256, tm=512,
                                 ret_nb=1, ret_nc=4, conv_bb=1, conv_rt=128)
    y_s, ret_s, conv_s = _stream(x_sample, pos_s, state_retention[layer], state_conv[layer], wts,
                                 ffn_tm=1024, tf=256, tm=512, ret_nb=4, ret_nc=1, conv_bb=16, conv_rt=ts)
    return (y_p, y_s, ret_p[None], conv_p[None], ret_s[None], conv_s[None])
```

```python
import functools

import jax
import jax.numpy as jnp
from jax import lax
from jax.experimental import pallas as pl
from jax.experimental.pallas import tpu as pltpu

F32 = jnp.float32
BF16 = jnp.bfloat16

RET_DK = 256
RET_DV = 256
RET_CHUNK = 128
CONV_WIDTH = 31
ROPE_BASE = 10000.0
EPS = 1e-6
PAST_LEN = 16384

LANES = 128
SUBLANES = 8
VMEM_LIMIT_BYTES = 60 * 1024 * 1024

ROPE_HALF = RET_DK // 2
CONV_HIST = CONV_WIDTH - 1
CONV_PAD = 32


def _rms(x, g):
    return x * lax.rsqrt(jnp.mean(x * x, axis=-1, keepdims=True) + EPS) * g


def _params(*semantics):
    return pltpu.CompilerParams(dimension_semantics=semantics,
                                vmem_limit_bytes=VMEM_LIMIT_BYTES)


def _resident(shape):
    nd = len(shape)
    return pl.BlockSpec(shape, lambda *_: (0,) * nd, pipeline_mode=pl.Buffered(1))


def _ffn_body(x_ref, g_ref, w1_ref, w3_ref, w2_ref, gf_ref, o_ref, h_ref, *, final_norm):
    j = pl.program_id(1)

    @pl.when(j == 0)
    def _():
        h_ref[...] = _rms(x_ref[...], g_ref[...]).astype(BF16)
        o_ref[...] = jnp.zeros_like(o_ref)

    h = h_ref[...]
    a = jnp.dot(h, w1_ref[...].astype(BF16), preferred_element_type=F32)
    b = jnp.dot(h, w3_ref[...].astype(BF16), preferred_element_type=F32)
    act = (jax.nn.silu(a) * b).astype(BF16)
    o_ref[...] += jnp.dot(act, w2_ref[...].astype(BF16), preferred_element_type=F32)

    @pl.when(j == pl.num_programs(1) - 1)
    def _():
        y = x_ref[...] + 0.5 * o_ref[...]
        if final_norm:
            y = _rms(y, gf_ref[...])
        o_ref[...] = y


def _ffn(x, g, w1, w3, w2, gf, *, final_norm, tm, tf):
    n, d = x.shape
    f = w1.shape[1]
    return pl.pallas_call(
        functools.partial(_ffn_body, final_norm=final_norm),
        grid=(n // tm, f // tf),
        in_specs=[
            pl.BlockSpec((tm, d), lambda i, j: (i, 0)),
            pl.BlockSpec((1, d), lambda i, j: (0, 0)),
            pl.BlockSpec((d, tf), lambda i, j: (0, j)),
            pl.BlockSpec((d, tf), lambda i, j: (0, j)),
            pl.BlockSpec((tf, d), lambda i, j: (j, 0)),
            pl.BlockSpec((1, d), lambda i, j: (0, 0)),
        ],
        out_specs=pl.BlockSpec((tm, d), lambda i, j: (i, 0)),
        out_shape=jax.ShapeDtypeStruct((n, d), F32),
        scratch_shapes=[pltpu.VMEM((tm, d), BF16)],
        compiler_params=_params("parallel", "arbitrary"),
        name="ffn",
    )(x, g, w1, w3, w2, gf)


def _inproj_body(x_ref, g_ref, w_ref, cos_ref, sin_ref,
                 q_ref, k_ref, v_ref, gate_ref, u_ref, *, n_heads, d_ret, d_conv):
    h = _rms(x_ref[...], g_ref[...]).astype(BF16)
    cos = cos_ref[...]
    sin = sin_ref[...]

    def proj(col, width):
        return jnp.dot(h, w_ref[:, col:col + width], preferred_element_type=F32)

    def rotary(t):
        t1, t2 = t[:, :ROPE_HALF], t[:, ROPE_HALF:]
        return jnp.concatenate([t1 * cos - t2 * sin, t2 * cos + t1 * sin], axis=-1)

    dq = n_heads * RET_DK
    for hd in range(n_heads):
        lo = hd * RET_DK
        q_ref[:, lo:lo + RET_DK] = (rotary(proj(lo, RET_DK)) * (RET_DK ** -0.5)).astype(BF16)
        k_ref[:, lo:lo + RET_DK] = rotary(proj(dq + lo, RET_DK)).astype(BF16)
    v_ref[...] = proj(2 * dq, d_ret).astype(BF16)
    gate_ref[...] = proj(2 * dq + d_ret, d_ret).astype(BF16)
    a = proj(2 * dq + 2 * d_ret, d_conv)
    b = proj(2 * dq + 2 * d_ret + d_conv, d_conv)
    u_ref[...] = a * jax.nn.sigmoid(b)


def _inproj(x, g, w_in, cos, sin, *, n_heads, d_ret, d_conv, tm):
    n, d = x.shape
    period = cos.shape[0] // tm
    dq = n_heads * RET_DK
    row = lambda i: (i, 0)
    return pl.pallas_call(
        functools.partial(_inproj_body, n_heads=n_heads, d_ret=d_ret, d_conv=d_conv),
        grid=(n // tm,),
        in_specs=[
            pl.BlockSpec((tm, d), row),
            _resident((1, d)),
            _resident(w_in.shape),
            pl.BlockSpec((tm, ROPE_HALF), lambda i: (i % period, 0)),
            pl.BlockSpec((tm, ROPE_HALF), lambda i: (i % period, 0)),
        ],
        out_specs=[
            pl.BlockSpec((tm, dq), row),
            pl.BlockSpec((tm, dq), row),
            pl.BlockSpec((tm, d_ret), row),
            pl.BlockSpec((tm, d_ret), row),
            pl.BlockSpec((tm, d_conv), row),
        ],
        out_shape=[
            jax.ShapeDtypeStruct((n, dq), BF16),
            jax.ShapeDtypeStruct((n, dq), BF16),
            jax.ShapeDtypeStruct((n, d_ret), BF16),
            jax.ShapeDtypeStruct((n, d_ret), BF16),
            jax.ShapeDtypeStruct((n, d_conv), F32),
        ],
        compiler_params=_params("parallel"),
        name="inproj",
    )(x, g, w_in, cos, sin)


def _retention_body(q_ref, k_ref, v_ref, gate_ref, s0_ref, dmask_ref, qdec_ref, kdec_ref,
                    cdec_ref, gng_ref, gnb_ref, y_ref, so_ref, *, n_heads, nb, nc, c):
    @pl.when(pl.program_id(1) == 0)
    def _():
        so_ref[...] = s0_ref[...]

    narrow = c % (2 * SUBLANES) != 0
    mm = F32 if narrow else BF16
    nt = (((1,), (1,)), ((), ()))
    tn = (((0,), (0,)), ((), ()))

    for hd in range(n_heads):
        cols = slice(hd * RET_DK, (hd + 1) * RET_DK)
        vcols = slice(hd * RET_DV, (hd + 1) * RET_DV)
        dmask = dmask_ref[hd]
        qdec = qdec_ref[hd]
        kdec = kdec_ref[hd]
        cdec = cdec_ref[hd]
        gng = gng_ref[:, vcols]
        gnb = gnb_ref[:, vcols]
        if narrow:
            q_all = q_ref[:, cols].astype(F32)
            k_all = k_ref[:, cols].astype(F32)
            v_all = v_ref[:, vcols].astype(F32)
            gate_all = gate_ref[:, vcols].astype(F32)
        for bb in range(nb):
            state = so_ref[bb, hd]
            for cc in range(nc):
                rows = slice((bb * nc + cc) * c, (bb * nc + cc + 1) * c)
                if narrow:
                    q, k, v, gate = q_all[rows], k_all[rows], v_all[rows], gate_all[rows]
                else:
                    q, k, v = q_ref[rows, cols], k_ref[rows, cols], v_ref[rows, vcols]
                    gate = gate_ref[rows, vcols].astype(F32)
                s = lax.dot_general(q, k, nt, preferred_element_type=F32) * dmask
                inner = jnp.dot(s.astype(mm), v, preferred_element_type=F32)
                qd = (q.astype(F32) * qdec).astype(mm)
                cross = jnp.dot(qd, state.astype(mm), preferred_element_type=F32)
                o = inner + cross
                kd = (k.astype(F32) * kdec).astype(mm)
                state = state * cdec + lax.dot_general(kd, v, tn, preferred_element_type=F32)
                mu = jnp.mean(o, axis=-1, keepdims=True)
                var = jnp.mean(jnp.square(o - mu), axis=-1, keepdims=True)
                on = (o - mu) * lax.rsqrt(var + EPS) * gng + gnb
                y_ref[rows, vcols] = (jax.nn.silu(gate) * on).astype(y_ref.dtype)
            so_ref[bb, hd] = state


def _retention(q, k, v, gate, state0, gng, gnb, *, layer, seq, c, nb, nc):
    n, d_ret = v.shape
    batch, n_heads = state0.shape[1:3]
    steps = seq // (nc * c)
    assert nb == 1 or steps == 1

    lg = jnp.log1p(-jnp.exp2(-5.0 - jnp.arange(n_heads, dtype=F32)))
    idx = jnp.arange(c, dtype=F32)
    diff = idx[:, None] - idx[None, :]
    causal = diff >= 0
    dmask = jnp.where(causal[None], jnp.exp(jnp.where(causal, diff, 0.0)[None] * lg[:, None, None]), 0.0)
    qdec = jnp.broadcast_to(jnp.exp((idx[None, :] + 1.0) * lg[:, None])[:, :, None], (n_heads, c, RET_DK))
    kdec = jnp.broadcast_to(jnp.exp((c - 1.0 - idx[None, :]) * lg[:, None])[:, :, None], (n_heads, c, RET_DK))
    cdec = jnp.broadcast_to(jnp.exp(c * lg)[:, None, None], (n_heads, 1, RET_DV))

    rows = nb * nc * c
    row = lambda b, t: (b * steps + t, 0)
    st_blk = (None, nb) + state0.shape[2:]
    y_dtype = BF16 if c % (2 * SUBLANES) == 0 else F32
    return pl.pallas_call(
        functools.partial(_retention_body, n_heads=n_heads, nb=nb, nc=nc, c=c),
        grid=(batch // nb, steps),
        in_specs=[
            pl.BlockSpec((rows, q.shape[1]), row),
            pl.BlockSpec((rows, k.shape[1]), row),
            pl.BlockSpec((rows, d_ret), row),
            pl.BlockSpec((rows, d_ret), row),
            pl.BlockSpec(st_blk, lambda b, t: (layer, b, 0, 0, 0)),
            _resident(dmask.shape),
            _resident(qdec.shape),
            _resident(kdec.shape),
            _resident(cdec.shape),
            _resident(gng.shape),
            _resident(gnb.shape),
        ],
        out_specs=[
            pl.BlockSpec((rows, d_ret), row),
            pl.BlockSpec(st_blk, lambda b, t: (0, b, 0, 0, 0)),
        ],
        out_shape=[
            jax.ShapeDtypeStruct((n, d_ret), y_dtype),
            jax.ShapeDtypeStruct((1,) + state0.shape[1:], state0.dtype),
        ],
        compiler_params=_params("parallel", "arbitrary"),
        name="retention",
    )(q, k, v, gate, state0, dmask, qdec, kdec, cdec, gng, gnb)


def _conv_body(u_ref, buf_ref, w_ref, b_ref, lng_ref, lnb_ref, y_ref, nbuf_ref, full_ref, cv_ref,
               *, bb, seq, rt):
    d = u_ref.shape[-1]
    n_lc = d // LANES
    head = CONV_PAD - CONV_HIST
    seg = seq + CONV_PAD

    def conv_tile(base, offsets):
        windows = {}
        for g, off in enumerate(offsets):
            for j in range(CONV_WIDTH):
                windows.setdefault(off + j, []).append((g, j))

        def conv_lanes(lc, carry):
            taps = [jnp.broadcast_to(w_ref[lc, j:j + 1, :], (SUBLANES, LANES))
                    for j in range(CONV_WIDTH)]
            accs = [jnp.zeros((SUBLANES, LANES), F32) for _ in offsets]
            for woff, uses in windows.items():
                xo = full_ref[lc, pl.ds(base + (head + woff), SUBLANES), :]
                for g, j in uses:
                    accs[g] = accs[g] + xo * taps[j]
            cv_ref[lc] = jnp.concatenate(accs, axis=0)
            return carry

        lax.fori_loop(0, n_lc, conv_lanes, 0)
        cv = jnp.concatenate([cv_ref[lc] for lc in range(n_lc)], axis=-1) + b_ref[...]
        mu = jnp.mean(cv, axis=-1, keepdims=True)
        var = jnp.mean(jnp.square(cv - mu), axis=-1, keepdims=True)
        ln = (cv - mu) * lax.rsqrt(var + EPS) * lng_ref[...] + lnb_ref[...]
        return jax.nn.silu(ln).astype(y_ref.dtype)

    for b in range(bb):
        for lc in range(n_lc):
            lanes = slice(lc * LANES, (lc + 1) * LANES)
            full_ref[lc, b * seg:b * seg + SUBLANES, :] = jnp.zeros((SUBLANES, LANES), F32)
            full_ref[lc, b * seg + head:b * seg + CONV_PAD, :] = buf_ref[b, :, lanes]
            full_ref[lc, b * seg + CONV_PAD:(b + 1) * seg, :] = u_ref[b, :, lanes]

    if seq == SUBLANES:
        y = conv_tile(0, [b * seg for b in range(bb)])
        y_ref[...] = y.reshape(bb, seq, d)
    else:
        assert bb == 1
        offsets = [r * SUBLANES for r in range(rt // SUBLANES)]

        def step(i, carry):
            t0 = pl.multiple_of(i * rt, rt)
            y_ref[0, pl.ds(t0, rt), :] = conv_tile(t0, offsets)
            return carry

        lax.fori_loop(0, seq // rt, step, 0)

    for b in range(bb):
        for lc in range(n_lc):
            lanes = slice(lc * LANES, (lc + 1) * LANES)
            nbuf_ref[b, :, lanes] = full_ref[lc, b * seg + seq + head:(b + 1) * seg, :]


def _conv(u, buf, w, b, lng, lnb, *, layer, bb, rt):
    batch, seq, d = u.shape
    assert rt == bb * seq if seq == SUBLANES else seq % rt == 0
    w = w.reshape(w.shape[0], d // LANES, LANES).transpose(1, 0, 2)
    y_dtype = BF16 if seq % (2 * SUBLANES) == 0 else F32
    blk = lambda i: (i, 0, 0)
    hist = lambda i: (layer, i, 0, 0)
    return pl.pallas_call(
        functools.partial(_conv_body, bb=bb, seq=seq, rt=rt),
        grid=(batch // bb,),
        in_specs=[
            pl.BlockSpec((bb, seq, d), blk),
            pl.BlockSpec((None, bb, CONV_HIST, d), hist),
            _resident(w.shape),
            _resident(b.shape),
            _resident(lng.shape),
            _resident(lnb.shape),
        ],
        out_specs=[
            pl.BlockSpec((bb, seq, d), blk),
            pl.BlockSpec((None, bb, CONV_HIST, d), lambda i: (0, i, 0, 0)),
        ],
        out_shape=[
            jax.ShapeDtypeStruct((batch, seq, d), y_dtype),
            jax.ShapeDtypeStruct((1, batch, CONV_HIST, d), u.dtype),
        ],
        scratch_shapes=[pltpu.VMEM((d // LANES, bb * (seq + CONV_PAD), LANES), F32),
                        pltpu.VMEM((d // LANES, rt, LANES), F32)],
        compiler_params=_params("parallel"),
        name="conv",
    )(u, buf, w, b, lng, lnb)


def _outproj_body(x_ref, r_ref, c_ref, w_ref, o_ref, *, d_ret):
    y = jnp.dot(r_ref[...].astype(BF16), w_ref[:d_ret, :], preferred_element_type=F32)
    y += jnp.dot(c_ref[...].astype(BF16), w_ref[d_ret:, :], preferred_element_type=F32)
    o_ref[...] = x_ref[...] + y


def _outproj(x, ret_y, conv_y, w_out, *, tm):
    n, d = x.shape
    d_ret, d_conv = ret_y.shape[1], conv_y.shape[1]
    row = lambda i: (i, 0)
    return pl.pallas_call(
        functools.partial(_outproj_body, d_ret=d_ret),
        grid=(n // tm,),
        in_specs=[
            pl.BlockSpec((tm, d), row),
            pl.BlockSpec((tm, d_ret), row),
            pl.BlockSpec((tm, d_conv), row),
            _resident(w_out.shape),
        ],
        out_specs=pl.BlockSpec((tm, d), row),
        out_shape=jax.ShapeDtypeStruct((n, d), F32),
        compiler_params=_params("parallel"),
        name="outproj",
    )(x, ret_y, conv_y, w_out)


def _rope_tables(pos, rows):
    inv_freq = ROPE_BASE ** (-jnp.arange(ROPE_HALF, dtype=F32) / ROPE_HALF)
    ang = pos[:, None] * inv_freq[None, :]
    reps = max(1, rows // pos.shape[0])
    return jnp.tile(jnp.cos(ang), (reps, 1)), jnp.tile(jnp.sin(ang), (reps, 1))


def _stream(x, pos, ret_state, conv_buf, wts, *, layer, ffn_tm, tf, tm, ret_nb, ret_nc,
            conv_bb, conv_rt):
    batch, seq, d = x.shape
    n_heads = ret_state.shape[2]
    d_ret = n_heads * RET_DV
    d_conv = conv_buf.shape[-1]
    c = min(seq, RET_CHUNK)
    xf = x.reshape(batch * seq, d)

    x1 = _ffn(xf, wts["norm_ffn1_g"], wts["ffn1_w1"], wts["ffn1_w3"], wts["ffn1_w2"],
              wts["norm_final_g"], final_norm=False, tm=ffn_tm, tf=tf)
    cos, sin = _rope_tables(pos, tm)
    q, k, v, gate, u = _inproj(x1, wts["norm_mix_g"], wts["w_in"], cos, sin,
                               n_heads=n_heads, d_ret=d_ret, d_conv=d_conv, tm=tm)
    ret_y, new_ret = _retention(q, k, v, gate, ret_state, wts["ret_gn_g"], wts["ret_gn_b"],
                                layer=layer, seq=seq, c=c, nb=ret_nb, nc=ret_nc)
    conv_y, new_buf = _conv(u.reshape(batch, seq, d_conv), conv_buf, wts["conv_w"], wts["conv_b"],
                            wts["conv_ln_g"], wts["conv_ln_b"], layer=layer, bb=conv_bb, rt=conv_rt)
    x2 = _outproj(x1, ret_y, conv_y.reshape(batch * seq, d_conv), wts["w_out"], tm=tm)
    y = _ffn(x2, wts["norm_ffn2_g"], wts["ffn2_w1"], wts["ffn2_w3"], wts["ffn2_w2"],
             wts["norm_final_g"], final_norm=True, tm=ffn_tm, tf=tf)
    return y.reshape(batch, seq, d), new_ret, new_buf


def kernel(x_prompt, x_sample, state_retention, state_conv, norm_ffn1_g, ffn1_w1, ffn1_w3, ffn1_w2, norm_mix_g, w_in, ret_gn_g, ret_gn_b, conv_w, conv_b, conv_ln_g, conv_ln_b, w_out, norm_ffn2_g, ffn2_w1, ffn2_w3, ffn2_w2, norm_final_g):
    depth = w_in.shape[0]
    assert depth == 1
    layer = 0
    row = lambda g: g.reshape(1, -1)
    wts = {
        "norm_ffn1_g": row(norm_ffn1_g[layer]),
        "ffn1_w1": ffn1_w1[layer],
        "ffn1_w3": ffn1_w3[layer],
        "ffn1_w2": ffn1_w2[layer],
        "norm_mix_g": row(norm_mix_g[layer]),
        "w_in": w_in[layer].astype(BF16),
        "ret_gn_g": row(ret_gn_g[layer]),
        "ret_gn_b": row(ret_gn_b[layer]),
        "conv_w": conv_w[layer],
        "conv_b": row(conv_b[layer]),
        "conv_ln_g": row(conv_ln_g[layer]),
        "conv_ln_b": row(conv_ln_b[layer]),
        "w_out": w_out[layer].astype(BF16),
        "norm_ffn2_g": row(norm_ffn2_g[layer]),
        "ffn2_w1": ffn2_w1[layer],
        "ffn2_w3": ffn2_w3[layer],
        "ffn2_w2": ffn2_w2[layer],
        "norm_final_g": row(norm_final_g),
    }
    bp, tp, _ = x_prompt.shape
    bs, ts, _ = x_sample.shape
    n_heads = state_retention.shape[2]
    d_conv = state_conv.shape[-1]

    pos_p = jnp.arange(tp, dtype=F32)
    pos_s = PAST_LEN + jnp.arange(ts, dtype=F32)
    r0 = jnp.zeros((1, bp, n_heads, RET_DK, RET_DV), state_retention.dtype)
    c0 = jnp.zeros((1, bp, CONV_HIST, d_conv), x_prompt.dtype)

    y_p, ret_p, conv_p = _stream(x_prompt, pos_p, r0, c0, wts, layer=0, ffn_tm=1024, tf=256, tm=512,
                                 ret_nb=1, ret_nc=4, conv_bb=1, conv_rt=128)
    conv_bb_s = 16
    y_s, ret_s, conv_s = _stream(x_sample, pos_s, state_retention, state_conv, wts, layer=layer,
                                 ffn_tm=1024, tf=256, tm=512, ret_nb=4, ret_nc=1,
                                 conv_bb=conv_bb_s, conv_rt=conv_bb_s * ts)
    return (y_p, y_s, ret_p, conv_p, ret_s, conv_s)
```

```python
import functools

import jax
import jax.numpy as jnp
from jax import lax
from jax.experimental import pallas as pl
from jax.experimental.pallas import tpu as pltpu

F32 = jnp.float32
BF16 = jnp.bfloat16

RET_DK = 256
RET_DV = 256
RET_CHUNK = 128
CONV_WIDTH = 31
ROPE_BASE = 10000.0
EPS = 1e-6
PAST_LEN = 16384

LANES = 128
SUBLANES = 8
VMEM_LIMIT_BYTES = 60 * 1024 * 1024

ROPE_HALF = RET_DK // 2
CONV_HIST = CONV_WIDTH - 1
CONV_PAD = 32
ROW_CHUNK = 128


def _rms(x, g):
    return x * lax.rsqrt(jnp.mean(x * x, axis=-1, keepdims=True) + EPS) * g


def _params(*semantics):
    return pltpu.CompilerParams(dimension_semantics=semantics,
                                vmem_limit_bytes=VMEM_LIMIT_BYTES)


def _resident(shape):
    nd = len(shape)
    return pl.BlockSpec(shape, lambda *_: (0,) * nd, pipeline_mode=pl.Buffered(1))


def _ffn_body(x_ref, g_ref, w1_ref, w3_ref, w2_ref, gf_ref, o_ref, *rest, final_norm, emit_bf16):
    h_ref = rest[-1]
    j = pl.program_id(1)

    def for_row_chunks(fn):
        def step(r, carry):
            fn(pl.ds(pl.multiple_of(r * ROW_CHUNK, ROW_CHUNK), ROW_CHUNK))
            return carry
        lax.fori_loop(0, x_ref.shape[0] // ROW_CHUNK, step, 0)

    @pl.when(j == 0)
    def _():
        def init(rows):
            h_ref[rows, :] = _rms(x_ref[rows, :], g_ref[...]).astype(BF16)
            o_ref[rows, :] = jnp.zeros((ROW_CHUNK, o_ref.shape[1]), F32)
        for_row_chunks(init)

    h = h_ref[...]
    w1, w3, w2 = (w_ref[...].astype(BF16) for w_ref in (w1_ref, w3_ref, w2_ref))
    if emit_bf16:
        for w_out_ref, w in zip(rest[:3], (w1, w3, w2)):
            w_out_ref[...] = w
    a = jnp.dot(h, w1, preferred_element_type=F32)
    b = jnp.dot(h, w3, preferred_element_type=F32)
    act = (jax.nn.silu(a) * b).astype(BF16)
    o_ref[...] += jnp.dot(act, w2, preferred_element_type=F32)

    @pl.when(j == pl.num_programs(1) - 1)
    def _():
        def finish(rows):
            y = x_ref[rows, :] + 0.5 * o_ref[rows, :]
            if final_norm:
                y = _rms(y, gf_ref[...])
            o_ref[rows, :] = y
        for_row_chunks(finish)


def _ffn(x, g, w1, w3, w2, gf, *, final_norm, tm, tf, emit_bf16=False):
    n, d = x.shape
    f = w1.shape[1]
    up = pl.BlockSpec((d, tf), lambda i, j: (0, j))
    down = pl.BlockSpec((tf, d), lambda i, j: (j, 0))
    out_specs = [pl.BlockSpec((tm, d), lambda i, j: (i, 0))]
    out_shape = [jax.ShapeDtypeStruct((n, d), F32)]
    if emit_bf16:
        out_specs += [up, up, down]
        out_shape += [jax.ShapeDtypeStruct(w.shape, BF16) for w in (w1, w3, w2)]
    outs = pl.pallas_call(
        functools.partial(_ffn_body, final_norm=final_norm, emit_bf16=emit_bf16),
        grid=(n // tm, f // tf),
        in_specs=[
            pl.BlockSpec((tm, d), lambda i, j: (i, 0)),
            pl.BlockSpec((1, d), lambda i, j: (0, 0)),
            up,
            up,
            down,
            pl.BlockSpec((1, d), lambda i, j: (0, 0)),
        ],
        out_specs=out_specs,
        out_shape=out_shape,
        scratch_shapes=[pltpu.VMEM((tm, d), BF16)],
        compiler_params=_params("parallel", "arbitrary"),
        name="ffn",
    )(x, g, w1, w3, w2, gf)
    return outs if emit_bf16 else outs[0]


def _inproj_body(x_ref, g_ref, w_ref, cos_ref, sin_ref,
                 q_ref, k_ref, v_ref, gate_ref, u_ref, *, n_heads, d_ret, d_conv):
    h = _rms(x_ref[...], g_ref[...]).astype(BF16)
    cos = cos_ref[...]
    sin = sin_ref[...]

    def proj(col, width):
        return jnp.dot(h, w_ref[:, col:col + width], preferred_element_type=F32)

    def rotary(t):
        t1, t2 = t[:, :ROPE_HALF], t[:, ROPE_HALF:]
        return jnp.concatenate([t1 * cos - t2 * sin, t2 * cos + t1 * sin], axis=-1)

    dq = n_heads * RET_DK
    for hd in range(n_heads):
        lo = hd * RET_DK
        q_ref[:, lo:lo + RET_DK] = (rotary(proj(lo, RET_DK)) * (RET_DK ** -0.5)).astype(BF16)
        k_ref[:, lo:lo + RET_DK] = rotary(proj(dq + lo, RET_DK)).astype(BF16)
    v_ref[...] = proj(2 * dq, d_ret).astype(BF16)
    gate_ref[...] = proj(2 * dq + d_ret, d_ret).astype(BF16)
    a = proj(2 * dq + 2 * d_ret, d_conv)
    b = proj(2 * dq + 2 * d_ret + d_conv, d_conv)
    u_ref[...] = a * jax.nn.sigmoid(b)


def _inproj(x, g, w_in, cos, sin, *, n_heads, d_ret, d_conv, tm):
    n, d = x.shape
    period = cos.shape[0] // tm
    dq = n_heads * RET_DK
    row = lambda i: (i, 0)
    return pl.pallas_call(
        functools.partial(_inproj_body, n_heads=n_heads, d_ret=d_ret, d_conv=d_conv),
        grid=(n // tm,),
        in_specs=[
            pl.BlockSpec((tm, d), row),
            _resident((1, d)),
            _resident(w_in.shape),
            pl.BlockSpec((tm, ROPE_HALF), lambda i: (i % period, 0)),
            pl.BlockSpec((tm, ROPE_HALF), lambda i: (i % period, 0)),
        ],
        out_specs=[
            pl.BlockSpec((tm, dq), row),
            pl.BlockSpec((tm, dq), row),
            pl.BlockSpec((tm, d_ret), row),
            pl.BlockSpec((tm, d_ret), row),
            pl.BlockSpec((tm, d_conv), row),
        ],
        out_shape=[
            jax.ShapeDtypeStruct((n, dq), BF16),
            jax.ShapeDtypeStruct((n, dq), BF16),
            jax.ShapeDtypeStruct((n, d_ret), BF16),
            jax.ShapeDtypeStruct((n, d_ret), BF16),
            jax.ShapeDtypeStruct((n, d_conv), F32),
        ],
        compiler_params=_params("parallel"),
        name="inproj",
    )(x, g, w_in, cos, sin)


def _retention_body(q_ref, k_ref, v_ref, gate_ref, s0_ref, dmask_ref, qdec_ref, kdec_ref,
                    cdec_ref, gng_ref, gnb_ref, y_ref, so_ref, *, n_heads, nb, nc, c, steps):
    if steps == 1:
        state_ref = s0_ref
    else:
        state_ref = so_ref

        @pl.when(pl.program_id(1) == 0)
        def _():
            so_ref[...] = s0_ref[...]

    narrow = c % (2 * SUBLANES) != 0
    mm = F32 if narrow else BF16
    nt = (((1,), (1,)), ((), ()))
    tn = (((0,), (0,)), ((), ()))

    for hd in range(n_heads):
        cols = slice(hd * RET_DK, (hd + 1) * RET_DK)
        vcols = slice(hd * RET_DV, (hd + 1) * RET_DV)
        dmask = dmask_ref[hd]
        qdec = qdec_ref[hd]
        kdec = kdec_ref[hd]
        cdec = cdec_ref[hd]
        gng = gng_ref[:, vcols]
        gnb = gnb_ref[:, vcols]
        if narrow:
            q_all = q_ref[:, cols].astype(F32)
            k_all = k_ref[:, cols].astype(F32)
            v_all = v_ref[:, vcols].astype(F32)
            gate_all = gate_ref[:, vcols].astype(F32)
        for bb in range(nb):
            state = state_ref[bb, hd]
            for cc in range(nc):
                rows = slice((bb * nc + cc) * c, (bb * nc + cc + 1) * c)
                if narrow:
                    q, k, v, gate = q_all[rows], k_all[rows], v_all[rows], gate_all[rows]
                else:
                    q, k, v = q_ref[rows, cols], k_ref[rows, cols], v_ref[rows, vcols]
                    gate = gate_ref[rows, vcols].astype(F32)
                s = lax.dot_general(q, k, nt, preferred_element_type=F32) * dmask
                inner = jnp.dot(s.astype(mm), v, preferred_element_type=F32)
                qd = (q.astype(F32) * qdec).astype(mm)
                cross = jnp.dot(qd, state.astype(mm), preferred_element_type=F32)
                o = inner + cross
                kd = (k.astype(F32) * kdec).astype(mm)
                state = state * cdec + lax.dot_general(kd, v, tn, preferred_element_type=F32)
                mu = jnp.mean(o, axis=-1, keepdims=True)
                var = jnp.mean(jnp.square(o - mu), axis=-1, keepdims=True)
                on = (o - mu) * lax.rsqrt(var + EPS) * gng + gnb
                y_ref[rows, vcols] = (jax.nn.silu(gate) * on).astype(y_ref.dtype)
            so_ref[bb, hd] = state


def _retention(q, k, v, gate, state0, gng, gnb, *, layer, seq, c, nb, nc):
    n, d_ret = v.shape
    batch, n_heads = state0.shape[1:3]
    steps = seq // (nc * c)
    assert nb == 1 or steps == 1

    lg = jnp.log1p(-jnp.exp2(-5.0 - jnp.arange(n_heads, dtype=F32)))
    idx = jnp.arange(c, dtype=F32)
    diff = idx[:, None] - idx[None, :]
    causal = diff >= 0
    dmask = jnp.where(causal[None], jnp.exp(jnp.where(causal, diff, 0.0)[None] * lg[:, None, None]), 0.0)
    qdec = jnp.broadcast_to(jnp.exp((idx[None, :] + 1.0) * lg[:, None])[:, :, None], (n_heads, c, RET_DK))
    kdec = jnp.broadcast_to(jnp.exp((c - 1.0 - idx[None, :]) * lg[:, None])[:, :, None], (n_heads, c, RET_DK))
    cdec = jnp.broadcast_to(jnp.exp(c * lg)[:, None, None], (n_heads, 1, RET_DV))

    rows = nb * nc * c
    row = lambda b, t: (b * steps + t, 0)
    st_blk = (None, nb) + state0.shape[2:]
    y_dtype = BF16 if c % (2 * SUBLANES) == 0 else F32
    return pl.pallas_call(
        functools.partial(_retention_body, n_heads=n_heads, nb=nb, nc=nc, c=c, steps=steps),
        grid=(batch // nb, steps),
        in_specs=[
            pl.BlockSpec((rows, q.shape[1]), row),
            pl.BlockSpec((rows, k.shape[1]), row),
            pl.BlockSpec((rows, d_ret), row),
            pl.BlockSpec((rows, d_ret), row),
            pl.BlockSpec(st_blk, lambda b, t: (layer, b, 0, 0, 0)),
            _resident(dmask.shape),
            _resident(qdec.shape),
            _resident(kdec.shape),
            _resident(cdec.shape),
            _resident(gng.shape),
            _resident(gnb.shape),
        ],
        out_specs=[
            pl.BlockSpec((rows, d_ret), row),
            pl.BlockSpec(st_blk, lambda b, t: (0, b, 0, 0, 0)),
        ],
        out_shape=[
            jax.ShapeDtypeStruct((n, d_ret), y_dtype),
            jax.ShapeDtypeStruct((1,) + state0.shape[1:], state0.dtype),
        ],
        compiler_params=_params("parallel", "arbitrary"),
        name="retention",
    )(q, k, v, gate, state0, dmask, qdec, kdec, cdec, gng, gnb)


def _conv_body(u_ref, buf_ref, w_ref, b_ref, lng_ref, lnb_ref, y_ref, nbuf_ref, full_ref, cv_ref,
               *, bb, seq, rt):
    d = u_ref.shape[-1]
    n_lc = d // LANES
    head = CONV_PAD - CONV_HIST
    seg = seq + CONV_PAD

    def conv_tile(base, offsets):
        windows = {}
        for g, off in enumerate(offsets):
            for j in range(CONV_WIDTH):
                windows.setdefault(off + j, []).append((g, j))

        def conv_lanes(lc, carry):
            taps = [jnp.broadcast_to(w_ref[lc, j:j + 1, :], (SUBLANES, LANES))
                    for j in range(CONV_WIDTH)]
            accs = [jnp.zeros((SUBLANES, LANES), F32) for _ in offsets]
            for woff, uses in windows.items():
                xo = full_ref[lc, pl.ds(base + (head + woff), SUBLANES), :]
                for g, j in uses:
                    accs[g] = accs[g] + xo * taps[j]
            cv_ref[lc] = jnp.concatenate(accs, axis=0)
            return carry

        lax.fori_loop(0, n_lc, conv_lanes, 0)
        cv = jnp.concatenate([cv_ref[lc] for lc in range(n_lc)], axis=-1) + b_ref[...]
        mu = jnp.mean(cv, axis=-1, keepdims=True)
        var = jnp.mean(jnp.square(cv - mu), axis=-1, keepdims=True)
        ln = (cv - mu) * lax.rsqrt(var + EPS) * lng_ref[...] + lnb_ref[...]
        return jax.nn.silu(ln).astype(y_ref.dtype)

    for b in range(bb):
        for lc in range(n_lc):
            lanes = slice(lc * LANES, (lc + 1) * LANES)
            full_ref[lc, b * seg:b * seg + SUBLANES, :] = jnp.zeros((SUBLANES, LANES), F32)
            full_ref[lc, b * seg + head:b * seg + CONV_PAD, :] = buf_ref[b, :, lanes]
            full_ref[lc, b * seg + CONV_PAD:(b + 1) * seg, :] = u_ref[b, :, lanes]

    if seq == SUBLANES:
        y = conv_tile(0, [b * seg for b in range(bb)])
        y_ref[...] = y.reshape(bb, seq, d)
    else:
        assert bb == 1
        offsets = [r * SUBLANES for r in range(rt // SUBLANES)]

        def step(i, carry):
            t0 = pl.multiple_of(i * rt, rt)
            y_ref[0, pl.ds(t0, rt), :] = conv_tile(t0, offsets)
            return carry

        lax.fori_loop(0, seq // rt, step, 0)

    for b in range(bb):
        for lc in range(n_lc):
            lanes = slice(lc * LANES, (lc + 1) * LANES)
            nbuf_ref[b, :, lanes] = full_ref[lc, b * seg + seq + head:(b + 1) * seg, :]


def _conv(u, buf, w, b, lng, lnb, *, layer, bb, rt):
    batch, seq, d = u.shape
    assert rt == bb * seq if seq == SUBLANES else seq % rt == 0
    w = w.reshape(w.shape[0], d // LANES, LANES).transpose(1, 0, 2)
    y_dtype = BF16 if seq % (2 * SUBLANES) == 0 else F32
    blk = lambda i: (i, 0, 0)
    hist = lambda i: (layer, i, 0, 0)
    return pl.pallas_call(
        functools.partial(_conv_body, bb=bb, seq=seq, rt=rt),
        grid=(batch // bb,),
        in_specs=[
            pl.BlockSpec((bb, seq, d), blk),
            pl.BlockSpec((None, bb, CONV_HIST, d), hist),
            _resident(w.shape),
            _resident(b.shape),
            _resident(lng.shape),
            _resident(lnb.shape),
        ],
        out_specs=[
            pl.BlockSpec((bb, seq, d), blk),
            pl.BlockSpec((None, bb, CONV_HIST, d), lambda i: (0, i, 0, 0)),
        ],
        out_shape=[
            jax.ShapeDtypeStruct((batch, seq, d), y_dtype),
            jax.ShapeDtypeStruct((1, batch, CONV_HIST, d), u.dtype),
        ],
        scratch_shapes=[pltpu.VMEM((d // LANES, bb * (seq + CONV_PAD), LANES), F32),
                        pltpu.VMEM((d // LANES, rt, LANES), F32)],
        compiler_params=_params("parallel"),
        name="conv",
    )(u, buf, w, b, lng, lnb)


def _outproj_body(x_ref, r_ref, c_ref, w_ref, o_ref, *, d_ret):
    y = jnp.dot(r_ref[...].astype(BF16), w_ref[:d_ret, :], preferred_element_type=F32)
    y += jnp.dot(c_ref[...].astype(BF16), w_ref[d_ret:, :], preferred_element_type=F32)
    o_ref[...] = x_ref[...] + y


def _outproj(x, ret_y, conv_y, w_out, *, tm):
    n, d = x.shape
    d_ret, d_conv = ret_y.shape[1], conv_y.shape[1]
    row = lambda i: (i, 0)
    return pl.pallas_call(
        functools.partial(_outproj_body, d_ret=d_ret),
        grid=(n // tm,),
        in_specs=[
            pl.BlockSpec((tm, d), row),
            pl.BlockSpec((tm, d_ret), row),
            pl.BlockSpec((tm, d_conv), row),
            _resident(w_out.shape),
        ],
        out_specs=pl.BlockSpec((tm, d), row),
        out_shape=jax.ShapeDtypeStruct((n, d), F32),
        compiler_params=_params("parallel"),
        name="outproj",
    )(x, ret_y, conv_y, w_out)


def _rope_tables(pos, rows):
    inv_freq = ROPE_BASE ** (-jnp.arange(ROPE_HALF, dtype=F32) / ROPE_HALF)
    ang = pos[:, None] * inv_freq[None, :]
    reps = max(1, rows // pos.shape[0])
    return jnp.tile(jnp.cos(ang), (reps, 1)), jnp.tile(jnp.sin(ang), (reps, 1))


def _stream(x, pos, ret_state, conv_buf, wts, ffn1_w, ffn2_w, *, layer, emit_bf16, ffn_tm, tf, tm,
            ret_nb, ret_nc, conv_bb, conv_rt):
    batch, seq, d = x.shape
    n_heads = ret_state.shape[2]
    d_ret = n_heads * RET_DV
    d_conv = conv_buf.shape[-1]
    c = min(seq, RET_CHUNK)
    xf = x.reshape(batch * seq, d)

    x1 = _ffn(xf, wts["norm_ffn1_g"], *ffn1_w, wts["norm_final_g"], final_norm=False,
              tm=ffn_tm, tf=tf, emit_bf16=emit_bf16)
    if emit_bf16:
        x1, *ffn1_w = x1
    cos, sin = _rope_tables(pos, tm)
    q, k, v, gate, u = _inproj(x1, wts["norm_mix_g"], wts["w_in"], cos, sin,
                               n_heads=n_heads, d_ret=d_ret, d_conv=d_conv, tm=tm)
    ret_y, new_ret = _retention(q, k, v, gate, ret_state, wts["ret_gn_g"], wts["ret_gn_b"],
                                layer=layer, seq=seq, c=c, nb=ret_nb, nc=ret_nc)
    conv_y, new_buf = _conv(u.reshape(batch, seq, d_conv), conv_buf, wts["conv_w"], wts["conv_b"],
                            wts["conv_ln_g"], wts["conv_ln_b"], layer=layer, bb=conv_bb, rt=conv_rt)
    x2 = _outproj(x1, ret_y, conv_y.reshape(batch * seq, d_conv), wts["w_out"], tm=tm)
    y = _ffn(x2, wts["norm_ffn2_g"], *ffn2_w, wts["norm_final_g"], final_norm=True,
             tm=ffn_tm, tf=tf, emit_bf16=emit_bf16)
    if emit_bf16:
        y, *ffn2_w = y
    return y.reshape(batch, seq, d), new_ret, new_buf, tuple(ffn1_w), tuple(ffn2_w)


def kernel(x_prompt, x_sample, state_retention, state_conv, norm_ffn1_g, ffn1_w1, ffn1_w3, ffn1_w2, norm_mix_g, w_in, ret_gn_g, ret_gn_b, conv_w, conv_b, conv_ln_g, conv_ln_b, w_out, norm_ffn2_g, ffn2_w1, ffn2_w3, ffn2_w2, norm_final_g):
    depth = w_in.shape[0]
    assert depth == 1
    layer = 0
    row = lambda g: g.reshape(1, -1)
    wts = {
        "norm_ffn1_g": row(norm_ffn1_g[layer]),
        "norm_mix_g": row(norm_mix_g[layer]),
        "w_in": w_in[layer].astype(BF16),
        "ret_gn_g": row(ret_gn_g[layer]),
        "ret_gn_b": row(ret_gn_b[layer]),
        "conv_w": conv_w[layer],
        "conv_b": row(conv_b[layer]),
        "conv_ln_g": row(conv_ln_g[layer]),
        "conv_ln_b": row(conv_ln_b[layer]),
        "w_out": w_out[layer].astype(BF16),
        "norm_ffn2_g": row(norm_ffn2_g[layer]),
        "norm_final_g": row(norm_final_g),
    }
    bp, tp, _ = x_prompt.shape
    bs, ts, _ = x_sample.shape
    n_heads = state_retention.shape[2]
    d_conv = state_conv.shape[-1]

    pos_p = jnp.arange(tp, dtype=F32)
    pos_s = PAST_LEN + jnp.arange(ts, dtype=F32)
    r0 = jnp.zeros((1, bp, n_heads, RET_DK, RET_DV), state_retention.dtype)
    c0 = jnp.zeros((1, bp, CONV_HIST, d_conv), x_prompt.dtype)

    ffn1_w = (ffn1_w1[layer], ffn1_w3[layer], ffn1_w2[layer])
    ffn2_w = (ffn2_w1[layer], ffn2_w3[layer], ffn2_w2[layer])
    conv_bb_s = 16
    y_s, ret_s, conv_s, ffn1_w, ffn2_w = _stream(
        x_sample, pos_s, state_retention, state_conv, wts, ffn1_w, ffn2_w, layer=layer,
        emit_bf16=True, ffn_tm=1024, tf=256, tm=512, ret_nb=8, ret_nc=1,
        conv_bb=conv_bb_s, conv_rt=conv_bb_s * ts)
    y_p, ret_p, conv_p, _, _ = _stream(
        x_prompt, pos_p, r0, c0, wts, ffn1_w, ffn2_w, layer=0,
        emit_bf16=False, ffn_tm=1024, tf=512, tm=512, ret_nb=1, ret_nc=4, conv_bb=1, conv_rt=128)
    return (y_p, y_s, ret_p, conv_p, ret_s, conv_s)
```

```python
import functools

import jax
import jax.numpy as jnp
from jax import lax
from jax.experimental import pallas as pl
from jax.experimental.pallas import tpu as pltpu

F32 = jnp.float32
BF16 = jnp.bfloat16

RET_DK = 256
RET_DV = 256
RET_CHUNK = 128
CONV_WIDTH = 31
ROPE_BASE = 10000.0
EPS = 1e-6
PAST_LEN = 16384

LANES = 128
SUBLANES = 8
VMEM_LIMIT_BYTES = 60 * 1024 * 1024

ROPE_HALF = RET_DK // 2
CONV_HIST = CONV_WIDTH - 1
CONV_PAD = 32
ROW_CHUNK = 128
CAST_ROWS = 128


def _rms(x, g):
    return x * lax.rsqrt(jnp.mean(x * x, axis=-1, keepdims=True) + EPS) * g


def _params(*semantics):
    return pltpu.CompilerParams(dimension_semantics=semantics,
                                vmem_limit_bytes=VMEM_LIMIT_BYTES)


def _resident(shape):
    nd = len(shape)
    return pl.BlockSpec(shape, lambda *_: (0,) * nd, pipeline_mode=pl.Buffered(1))


def _ffn_body(x_ref, g_ref, w1_ref, w3_ref, w2_ref, gf_ref, *rest, final_norm, emit_bf16, n_cast):
    n_emit = 3 if emit_bf16 else 0
    cast_src = rest[:n_cast]
    o_ref = rest[n_cast]
    emit_refs = rest[n_cast + 1:n_cast + 1 + n_emit]
    cast_dst = rest[n_cast + 1 + n_emit:n_cast + 1 + n_emit + n_cast]
    h_ref = rest[-1]
    j = pl.program_id(1)

    def step(first):
        if first:
            h = _rms(x_ref[...], g_ref[...]).astype(BF16)
            h_ref[...] = h
        else:
            h = h_ref[...]
        w1, w3, w2 = (w_ref[...].astype(BF16) for w_ref in (w1_ref, w3_ref, w2_ref))
        for dst_ref, w in zip(emit_refs, (w1, w3, w2)):
            dst_ref[...] = w
        for src_ref, dst_ref in zip(cast_src, cast_dst):
            dst_ref[...] = src_ref[...].astype(BF16)
        a = jnp.dot(h, w1, preferred_element_type=F32)
        b = jnp.dot(h, w3, preferred_element_type=F32)
        act = (jax.nn.silu(a) * b).astype(BF16)
        part = jnp.dot(act, w2, preferred_element_type=F32)
        if first:
            o_ref[...] = part
        else:
            o_ref[...] += part

    @pl.when(j == 0)
    def _():
        step(True)

    @pl.when(j > 0)
    def _():
        step(False)

    @pl.when(j == pl.num_programs(1) - 1)
    def _():
        def finish(r, carry):
            rows = pl.ds(pl.multiple_of(r * ROW_CHUNK, ROW_CHUNK), ROW_CHUNK)
            y = x_ref[rows, :] + 0.5 * o_ref[rows, :]
            if final_norm:
                y = _rms(y, gf_ref[...])
            o_ref[rows, :] = y
            return carry
        lax.fori_loop(0, x_ref.shape[0] // ROW_CHUNK, finish, 0)


def _ffn(x, g, w1, w3, w2, gf, *, final_norm, tm, tf, emit_bf16=False, cast_jobs=()):
    n, d = x.shape
    f = w1.shape[1]
    steps = f // tf
    up = pl.BlockSpec((d, tf), lambda i, j: (0, j))
    down = pl.BlockSpec((tf, d), lambda i, j: (j, 0))
    out_specs = [pl.BlockSpec((tm, d), lambda i, j: (i, 0))]
    out_shape = [jax.ShapeDtypeStruct((n, d), F32)]
    if emit_bf16:
        out_specs += [up, up, down]
        out_shape += [jax.ShapeDtypeStruct(w.shape, BF16) for w in (w1, w3, w2)]
    cast_specs = []
    for m in cast_jobs:
        blocks = min(steps, m.shape[0] // CAST_ROWS)
        rows = m.shape[0] // blocks
        assert rows * blocks == m.shape[0] and rows % (2 * SUBLANES) == 0
        cast_specs.append(pl.BlockSpec((rows, m.shape[1]), lambda i, j, nb=blocks: (jnp.minimum(j, nb - 1), 0)))
        out_shape.append(jax.ShapeDtypeStruct(m.shape, BF16))
    outs = pl.pallas_call(
        functools.partial(_ffn_body, final_norm=final_norm, emit_bf16=emit_bf16, n_cast=len(cast_jobs)),
        grid=(n // tm, steps),
        in_specs=[
            pl.BlockSpec((tm, d), lambda i, j: (i, 0)),
            pl.BlockSpec((1, d), lambda i, j: (0, 0)),
            up,
            up,
            down,
            pl.BlockSpec((1, d), lambda i, j: (0, 0)),
        ] + cast_specs,
        out_specs=out_specs + cast_specs,
        out_shape=out_shape,
        scratch_shapes=[pltpu.VMEM((tm, d), BF16)],
        compiler_params=_params("parallel", "arbitrary"),
        name="ffn",
    )(x, g, w1, w3, w2, gf, *cast_jobs)
    return outs if len(outs) > 1 else outs[0]


def _inproj_body(x_ref, g_ref, w_ref, cos_ref, sin_ref,
                 q_ref, k_ref, v_ref, gate_ref, u_ref, *, n_heads, d_ret, d_conv):
    h = _rms(x_ref[...], g_ref[...]).astype(BF16)
    cos = cos_ref[...]
    sin = sin_ref[...]

    def proj(col, width):
        return jnp.dot(h, w_ref[:, col:col + width], preferred_element_type=F32)

    def rotary(t):
        t1, t2 = t[:, :ROPE_HALF], t[:, ROPE_HALF:]
        return jnp.concatenate([t1 * cos - t2 * sin, t2 * cos + t1 * sin], axis=-1)

    dq = n_heads * RET_DK
    for hd in range(n_heads):
        lo = hd * RET_DK
        q_ref[:, lo:lo + RET_DK] = (rotary(proj(lo, RET_DK)) * (RET_DK ** -0.5)).astype(BF16)
        k_ref[:, lo:lo + RET_DK] = rotary(proj(dq + lo, RET_DK)).astype(BF16)
    v_ref[...] = proj(2 * dq, d_ret).astype(BF16)
    gate_ref[...] = proj(2 * dq + d_ret, d_ret).astype(BF16)
    a = proj(2 * dq + 2 * d_ret, d_conv)
    b = proj(2 * dq + 2 * d_ret + d_conv, d_conv)
    u_ref[...] = a * jax.nn.sigmoid(b)


def _inproj(x, g, w_in, cos, sin, *, n_heads, d_ret, d_conv, tm):
    n, d = x.shape
    period = cos.shape[0] // tm
    dq = n_heads * RET_DK
    row = lambda i: (i, 0)
    return pl.pallas_call(
        functools.partial(_inproj_body, n_heads=n_heads, d_ret=d_ret, d_conv=d_conv),
        grid=(n // tm,),
        in_specs=[
            pl.BlockSpec((tm, d), row),
            _resident((1, d)),
            _resident(w_in.shape),
            pl.BlockSpec((tm, ROPE_HALF), lambda i: (i % period, 0)),
            pl.BlockSpec((tm, ROPE_HALF), lambda i: (i % period, 0)),
        ],
        out_specs=[
            pl.BlockSpec((tm, dq), row),
            pl.BlockSpec((tm, dq), row),
            pl.BlockSpec((tm, d_ret), row),
            pl.BlockSpec((tm, d_ret), row),
            pl.BlockSpec((tm, d_conv), row),
        ],
        out_shape=[
            jax.ShapeDtypeStruct((n, dq), BF16),
            jax.ShapeDtypeStruct((n, dq), BF16),
            jax.ShapeDtypeStruct((n, d_ret), BF16),
            jax.ShapeDtypeStruct((n, d_ret), BF16),
            jax.ShapeDtypeStruct((n, d_conv), F32),
        ],
        compiler_params=_params("parallel"),
        name="inproj",
    )(x, g, w_in, cos, sin)


def _retention_body(q_ref, k_ref, v_ref, gate_ref, s0_ref, dmask_ref, qdec_ref, kdec_ref,
                    cdec_ref, gng_ref, gnb_ref, y_ref, so_ref, *, n_heads, nb, nc, c, steps):
    if steps == 1:
        state_ref = s0_ref
    else:
        state_ref = so_ref

        @pl.when(pl.program_id(1) == 0)
        def _():
            so_ref[...] = s0_ref[...]

    narrow = c % (2 * SUBLANES) != 0
    mm = F32 if narrow else BF16
    nt = (((1,), (1,)), ((), ()))
    tn = (((0,), (0,)), ((), ()))

    for hd in range(n_heads):
        cols = slice(hd * RET_DK, (hd + 1) * RET_DK)
        vcols = slice(hd * RET_DV, (hd + 1) * RET_DV)
        dmask = dmask_ref[hd]
        qdec = qdec_ref[hd]
        kdec = kdec_ref[hd]
        cdec = cdec_ref[hd]
        gng = gng_ref[:, vcols]
        gnb = gnb_ref[:, vcols]
        if narrow:
            q_all = q_ref[:, cols].astype(F32)
            k_all = k_ref[:, cols].astype(F32)
            v_all = v_ref[:, vcols].astype(F32)
            gate_all = gate_ref[:, vcols].astype(F32)
        for bb in range(nb):
            state = state_ref[bb, hd]
            for cc in range(nc):
                rows = slice((bb * nc + cc) * c, (bb * nc + cc + 1) * c)
                if narrow:
                    q, k, v, gate = q_all[rows], k_all[rows], v_all[rows], gate_all[rows]
                else:
                    q, k, v = q_ref[rows, cols], k_ref[rows, cols], v_ref[rows, vcols]
                    gate = gate_ref[rows, vcols].astype(F32)
                s = lax.dot_general(q, k, nt, preferred_element_type=F32) * dmask
                inner = jnp.dot(s.astype(mm), v, preferred_element_type=F32)
                qd = (q.astype(F32) * qdec).astype(mm)
                cross = jnp.dot(qd, state.astype(mm), preferred_element_type=F32)
                o = inner + cross
                kd = (k.astype(F32) * kdec).astype(mm)
                state = state * cdec + lax.dot_general(kd, v, tn, preferred_element_type=F32)
                mu = jnp.mean(o, axis=-1, keepdims=True)
                var = jnp.mean(jnp.square(o - mu), axis=-1, keepdims=True)
                on = (o - mu) * lax.rsqrt(var + EPS) * gng + gnb
                y_ref[rows, vcols] = (jax.nn.silu(gate) * on).astype(y_ref.dtype)
            so_ref[bb, hd] = state


def _retention(q, k, v, gate, state0, gng, gnb, *, layer, seq, c, nb, nc):
    n, d_ret = v.shape
    batch, n_heads = state0.shape[1:3]
    steps = seq // (nc * c)
    assert nb == 1 or steps == 1

    lg = jnp.log1p(-jnp.exp2(-5.0 - jnp.arange(n_heads, dtype=F32)))
    idx = jnp.arange(c, dtype=F32)
    diff = idx[:, None] - idx[None, :]
    causal = diff >= 0
    dmask = jnp.where(causal[None], jnp.exp(jnp.where(causal, diff, 0.0)[None] * lg[:, None, None]), 0.0)
    qdec = jnp.broadcast_to(jnp.exp((idx[None, :] + 1.0) * lg[:, None])[:, :, None], (n_heads, c, RET_DK))
    kdec = jnp.broadcast_to(jnp.exp((c - 1.0 - idx[None, :]) * lg[:, None])[:, :, None], (n_heads, c, RET_DK))
    cdec = jnp.broadcast_to(jnp.exp(c * lg)[:, None, None], (n_heads, 1, RET_DV))

    rows = nb * nc * c
    row = lambda b, t: (b * steps + t, 0)
    st_blk = (None, nb) + state0.shape[2:]
    y_dtype = BF16 if c % (2 * SUBLANES) == 0 else F32
    return pl.pallas_call(
        functools.partial(_retention_body, n_heads=n_heads, nb=nb, nc=nc, c=c, steps=steps),
        grid=(batch // nb, steps),
        in_specs=[
            pl.BlockSpec((rows, q.shape[1]), row),
            pl.BlockSpec((rows, k.shape[1]), row),
            pl.BlockSpec((rows, d_ret), row),
            pl.BlockSpec((rows, d_ret), row),
            pl.BlockSpec(st_blk, lambda b, t: (layer, b, 0, 0, 0)),
            _resident(dmask.shape),
            _resident(qdec.shape),
            _resident(kdec.shape),
            _resident(cdec.shape),
            _resident(gng.shape),
            _resident(gnb.shape),
        ],
        out_specs=[
            pl.BlockSpec((rows, d_ret), row),
            pl.BlockSpec(st_blk, lambda b, t: (0, b, 0, 0, 0)),
        ],
        out_shape=[
            jax.ShapeDtypeStruct((n, d_ret), y_dtype),
            jax.ShapeDtypeStruct((1,) + state0.shape[1:], state0.dtype),
        ],
        compiler_params=_params("parallel", "arbitrary"),
        name="retention",
    )(q, k, v, gate, state0, dmask, qdec, kdec, cdec, gng, gnb)


def _conv_body(u_ref, buf_ref, w_ref, b_ref, lng_ref, lnb_ref, y_ref, nbuf_ref, full_ref, cv_ref,
               *, bb, seq, rt):
    d = u_ref.shape[-1]
    n_lc = d // LANES
    head = CONV_PAD - CONV_HIST
    seg = seq + CONV_PAD

    def conv_tile(base, offsets):
        windows = {}
        for g, off in enumerate(offsets):
            for j in range(CONV_WIDTH):
                windows.setdefault(off + j, []).append((g, j))

        def conv_lanes(lc, carry):
            taps = [jnp.broadcast_to(w_ref[lc, j:j + 1, :], (SUBLANES, LANES))
                    for j in range(CONV_WIDTH)]
            accs = [jnp.zeros((SUBLANES, LANES), F32) for _ in offsets]
            for woff, uses in windows.items():
                xo = full_ref[lc, pl.ds(base + (head + woff), SUBLANES), :]
                for g, j in uses:
                    accs[g] = accs[g] + xo * taps[j]
            cv_ref[lc] = jnp.concatenate(accs, axis=0)
            return carry

        lax.fori_loop(0, n_lc, conv_lanes, 0)
        cv = jnp.concatenate([cv_ref[lc] for lc in range(n_lc)], axis=-1) + b_ref[...]
        mu = jnp.mean(cv, axis=-1, keepdims=True)
        var = jnp.mean(jnp.square(cv - mu), axis=-1, keepdims=True)
        ln = (cv - mu) * lax.rsqrt(var + EPS) * lng_ref[...] + lnb_ref[...]
        return jax.nn.silu(ln).astype(y_ref.dtype)

    for b in range(bb):
        for lc in range(n_lc):
            lanes = slice(lc * LANES, (lc + 1) * LANES)
            full_ref[lc, b * seg:b * seg + SUBLANES, :] = jnp.zeros((SUBLANES, LANES), F32)
            full_ref[lc, b * seg + head:b * seg + CONV_PAD, :] = buf_ref[b, :, lanes]
            full_ref[lc, b * seg + CONV_PAD:(b + 1) * seg, :] = u_ref[b, :, lanes]

    if seq == SUBLANES:
        y = conv_tile(0, [b * seg for b in range(bb)])
        y_ref[...] = y.reshape(bb, seq, d)
    else:
        assert bb == 1
        offsets = [r * SUBLANES for r in range(rt // SUBLANES)]

        def step(i, carry):
            t0 = pl.multiple_of(i * rt, rt)
            y_ref[0, pl.ds(t0, rt), :] = conv_tile(t0, offsets)
            return carry

        lax.fori_loop(0, seq // rt, step, 0)

    for b in range(bb):
        for lc in range(n_lc):
            lanes = slice(lc * LANES, (lc + 1) * LANES)
            nbuf_ref[b, :, lanes] = full_ref[lc, b * seg + seq + head:(b + 1) * seg, :]


def _conv(u, buf, w, b, lng, lnb, *, layer, bb, rt):
    batch, seq, d = u.shape
    assert rt == bb * seq if seq == SUBLANES else seq % rt == 0
    w = w.reshape(w.shape[0], d // LANES, LANES).transpose(1, 0, 2)
    y_dtype = BF16 if seq % (2 * SUBLANES) == 0 else F32
    blk = lambda i: (i, 0, 0)
    hist = lambda i: (layer, i, 0, 0)
    return pl.pallas_call(
        functools.partial(_conv_body, bb=bb, seq=seq, rt=rt),
        grid=(batch // bb,),
        in_specs=[
            pl.BlockSpec((bb, seq, d), blk),
            pl.BlockSpec((None, bb, CONV_HIST, d), hist),
            _resident(w.shape),
            _resident(b.shape),
            _resident(lng.shape),
            _resident(lnb.shape),
        ],
        out_specs=[
            pl.BlockSpec((bb, seq, d), blk),
            pl.BlockSpec((None, bb, CONV_HIST, d), lambda i: (0, i, 0, 0)),
        ],
        out_shape=[
            jax.ShapeDtypeStruct((batch, seq, d), y_dtype),
            jax.ShapeDtypeStruct((1, batch, CONV_HIST, d), u.dtype),
        ],
        scratch_shapes=[pltpu.VMEM((d // LANES, bb * (seq + CONV_PAD), LANES), F32),
                        pltpu.VMEM((d // LANES, rt, LANES), F32)],
        compiler_params=_params("parallel"),
        name="conv",
    )(u, buf, w, b, lng, lnb)


def _outproj_body(x_ref, r_ref, c_ref, w_ref, o_ref, *, d_ret):
    y = jnp.dot(r_ref[...].astype(BF16), w_ref[:d_ret, :], preferred_element_type=F32)
    y += jnp.dot(c_ref[...].astype(BF16), w_ref[d_ret:, :], preferred_element_type=F32)
    o_ref[...] = x_ref[...] + y


def _outproj(x, ret_y, conv_y, w_out, *, tm):
    n, d = x.shape
    d_ret, d_conv = ret_y.shape[1], conv_y.shape[1]
    row = lambda i: (i, 0)
    return pl.pallas_call(
        functools.partial(_outproj_body, d_ret=d_ret),
        grid=(n // tm,),
        in_specs=[
            pl.BlockSpec((tm, d), row),
            pl.BlockSpec((tm, d_ret), row),
            pl.BlockSpec((tm, d_conv), row),
            _resident(w_out.shape),
        ],
        out_specs=pl.BlockSpec((tm, d), row),
        out_shape=jax.ShapeDtypeStruct((n, d), F32),
        compiler_params=_params("parallel"),
        name="outproj",
    )(x, ret_y, conv_y, w_out)


def _rope_tables(pos, rows):
    inv_freq = ROPE_BASE ** (-jnp.arange(ROPE_HALF, dtype=F32) / ROPE_HALF)
    ang = pos[:, None] * inv_freq[None, :]
    reps = max(1, rows // pos.shape[0])
    return jnp.tile(jnp.cos(ang), (reps, 1)), jnp.tile(jnp.sin(ang), (reps, 1))


def _stream(x, pos, ret_state, conv_buf, wts, ffn1_w, ffn2_w, *, layer, emit_bf16, ffn_tm, tf, tm,
            ret_nb, ret_nc, conv_bb, conv_rt):
    batch, seq, d = x.shape
    n_heads = ret_state.shape[2]
    d_ret = n_heads * RET_DV
    d_conv = conv_buf.shape[-1]
    c = min(seq, RET_CHUNK)
    xf = x.reshape(batch * seq, d)

    x1 = _ffn(xf, wts["norm_ffn1_g"], *ffn1_w, wts["norm_final_g"], final_norm=False,
              tm=ffn_tm, tf=tf, emit_bf16=emit_bf16,
              cast_jobs=(wts["w_in"], wts["w_out"]) if emit_bf16 else ())
    if emit_bf16:
        x1, *ffn1_w = x1
        *ffn1_w, w_in, w_out = ffn1_w
        wts = dict(wts, w_in=w_in, w_out=w_out)
    cos, sin = _rope_tables(pos, tm)
    q, k, v, gate, u = _inproj(x1, wts["norm_mix_g"], wts["w_in"], cos, sin,
                               n_heads=n_heads, d_ret=d_ret, d_conv=d_conv, tm=tm)
    ret_y, new_ret = _retention(q, k, v, gate, ret_state, wts["ret_gn_g"], wts["ret_gn_b"],
                                layer=layer, seq=seq, c=c, nb=ret_nb, nc=ret_nc)
    conv_y, new_buf = _conv(u.reshape(batch, seq, d_conv), conv_buf, wts["conv_w"], wts["conv_b"],
                            wts["conv_ln_g"], wts["conv_ln_b"], layer=layer, bb=conv_bb, rt=conv_rt)
    x2 = _outproj(x1, ret_y, conv_y.reshape(batch * seq, d_conv), wts["w_out"], tm=tm)
    y = _ffn(x2, wts["norm_ffn2_g"], *ffn2_w, wts["norm_final_g"], final_norm=True,
             tm=ffn_tm, tf=tf, emit_bf16=emit_bf16)
    if emit_bf16:
        y, *ffn2_w = y
    return y.reshape(batch, seq, d), new_ret, new_buf, wts, tuple(ffn1_w), tuple(ffn2_w)


def kernel(x_prompt, x_sample, state_retention, state_conv, norm_ffn1_g, ffn1_w1, ffn1_w3, ffn1_w2, norm_mix_g, w_in, ret_gn_g, ret_gn_b, conv_w, conv_b, conv_ln_g, conv_ln_b, w_out, norm_ffn2_g, ffn2_w1, ffn2_w3, ffn2_w2, norm_final_g):
    depth = w_in.shape[0]
    assert depth == 1
    layer = 0
    row = lambda g: g.reshape(1, -1)
    wts = {
        "norm_ffn1_g": row(norm_ffn1_g[layer]),
        "norm_mix_g": row(norm_mix_g[layer]),
        "w_in": w_in[layer],
        "ret_gn_g": row(ret_gn_g[layer]),
        "ret_gn_b": row(ret_gn_b[layer]),
        "conv_w": conv_w[layer],
        "conv_b": row(conv_b[layer]),
        "conv_ln_g": row(conv_ln_g[layer]),
        "conv_ln_b": row(conv_ln_b[layer]),
        "w_out": w_out[layer],
        "norm_ffn2_g": row(norm_ffn2_g[layer]),
        "norm_final_g": row(norm_final_g),
    }
    bp, tp, _ = x_prompt.shape
    bs, ts, _ = x_sample.shape
    n_heads = state_retention.shape[2]
    d_conv = state_conv.shape[-1]

    pos_p = jnp.arange(tp, dtype=F32)
    pos_s = PAST_LEN + jnp.arange(ts, dtype=F32)
    r0 = jnp.zeros((1, bp, n_heads, RET_DK, RET_DV), state_retention.dtype)
    c0 = jnp.zeros((1, bp, CONV_HIST, d_conv), x_prompt.dtype)

    ffn1_w = (ffn1_w1[layer], ffn1_w3[layer], ffn1_w2[layer])
    ffn2_w = (ffn2_w1[layer], ffn2_w3[layer], ffn2_w2[layer])
    conv_bb_s = 16
    y_s, ret_s, conv_s, wts, ffn1_w, ffn2_w = _stream(
        x_sample, pos_s, state_retention, state_conv, wts, ffn1_w, ffn2_w, layer=layer,
        emit_bf16=True, ffn_tm=1024, tf=256, tm=512, ret_nb=8, ret_nc=1,
        conv_bb=conv_bb_s, conv_rt=conv_bb_s * ts)
    y_p, ret_p, conv_p, _, _, _ = _stream(
        x_prompt, pos_p, r0, c0, wts, ffn1_w, ffn2_w, layer=0,
        emit_bf16=False, ffn_tm=1024, tf=512, tm=512, ret_nb=1, ret_nc=4, conv_bb=1, conv_rt=128)
    return (y_p, y_s, ret_p, conv_p, ret_s, conv_s)
```

```python
import functools

import jax
import jax.numpy as jnp
from jax import lax
from jax.experimental import pallas as pl
from jax.experimental.pallas import tpu as pltpu

F32 = jnp.float32
BF16 = jnp.bfloat16

RET_DK = 256
RET_DV = 256
RET_CHUNK = 128
CONV_WIDTH = 31
ROPE_BASE = 10000.0
EPS = 1e-6
PAST_LEN = 16384

LANES = 128
SUBLANES = 8
VMEM_LIMIT_BYTES = 60 * 1024 * 1024

ROPE_HALF = RET_DK // 2
CONV_HIST = CONV_WIDTH - 1
CONV_PAD = 32
ROW_CHUNK = 128
CAST_ROWS = 128


def _rms(x, g):
    return x * lax.rsqrt(jnp.mean(x * x, axis=-1, keepdims=True) + EPS) * g


def _params(*semantics):
    return pltpu.CompilerParams(dimension_semantics=semantics,
                                vmem_limit_bytes=VMEM_LIMIT_BYTES)


def _resident(shape):
    nd = len(shape)
    return pl.BlockSpec(shape, lambda *_: (0,) * nd, pipeline_mode=pl.Buffered(1))


def _ffn_body(x_ref, g_ref, w1_ref, w3_ref, w2_ref, gf_ref, *rest, final_norm, emit_bf16, n_cast):
    n_emit = 3 if emit_bf16 else 0
    cast_src = rest[:n_cast]
    o_ref = rest[n_cast]
    emit_refs = rest[n_cast + 1:n_cast + 1 + n_emit]
    cast_dst = rest[n_cast + 1 + n_emit:n_cast + 1 + n_emit + n_cast]
    h_ref = rest[-1]
    j = pl.program_id(1)

    def step(first):
        if first:
            h = _rms(x_ref[...], g_ref[...]).astype(BF16)
            h_ref[...] = h
        else:
            h = h_ref[...]
        w1, w3, w2 = (w_ref[...].astype(BF16) for w_ref in (w1_ref, w3_ref, w2_ref))
        for dst_ref, w in zip(emit_refs, (w1, w3, w2)):
            dst_ref[...] = w
        for src_ref, dst_ref in zip(cast_src, cast_dst):
            dst_ref[...] = src_ref[...].astype(BF16)
        a = jnp.dot(h, w1, preferred_element_type=F32)
        b = jnp.dot(h, w3, preferred_element_type=F32)
        act = (jax.nn.silu(a) * b).astype(BF16)
        part = jnp.dot(act, w2, preferred_element_type=F32)
        if first:
            o_ref[...] = part
        else:
            o_ref[...] += part

    @pl.when(j == 0)
    def _():
        step(True)

    @pl.when(j > 0)
    def _():
        step(False)

    @pl.when(j == pl.num_programs(1) - 1)
    def _():
        def finish(r, carry):
            rows = pl.ds(pl.multiple_of(r * ROW_CHUNK, ROW_CHUNK), ROW_CHUNK)
            y = x_ref[rows, :] + 0.5 * o_ref[rows, :]
            if final_norm:
                y = _rms(y, gf_ref[...])
            o_ref[rows, :] = y
            return carry
        lax.fori_loop(0, x_ref.shape[0] // ROW_CHUNK, finish, 0)


def _ffn(x, g, w1, w3, w2, gf, *, final_norm, tm, tf, emit_bf16=False, cast_jobs=()):
    n, d = x.shape
    f = w1.shape[1]
    steps = f // tf
    up = pl.BlockSpec((d, tf), lambda i, j: (0, j))
    down = pl.BlockSpec((tf, d), lambda i, j: (j, 0))
    out_specs = [pl.BlockSpec((tm, d), lambda i, j: (i, 0))]
    out_shape = [jax.ShapeDtypeStruct((n, d), F32)]
    if emit_bf16:
        out_specs += [up, up, down]
        out_shape += [jax.ShapeDtypeStruct(w.shape, BF16) for w in (w1, w3, w2)]
    cast_specs = []
    for m in cast_jobs:
        blocks = min(steps, m.shape[0] // CAST_ROWS)
        rows = m.shape[0] // blocks
        assert rows * blocks == m.shape[0] and rows % (2 * SUBLANES) == 0
        cast_specs.append(pl.BlockSpec((rows, m.shape[1]), lambda i, j, nb=blocks: (jnp.minimum(j, nb - 1), 0)))
        out_shape.append(jax.ShapeDtypeStruct(m.shape, BF16))
    outs = pl.pallas_call(
        functools.partial(_ffn_body, final_norm=final_norm, emit_bf16=emit_bf16, n_cast=len(cast_jobs)),
        grid=(n // tm, steps),
        in_specs=[
            pl.BlockSpec((tm, d), lambda i, j: (i, 0)),
            pl.BlockSpec((1, d), lambda i, j: (0, 0)),
            up,
            up,
            down,
            pl.BlockSpec((1, d), lambda i, j: (0, 0)),
        ] + cast_specs,
        out_specs=out_specs + cast_specs,
        out_shape=out_shape,
        scratch_shapes=[pltpu.VMEM((tm, d), BF16)],
        compiler_params=_params("parallel", "arbitrary"),
        name="ffn",
    )(x, g, w1, w3, w2, gf, *cast_jobs)
    return outs if len(outs) > 1 else outs[0]


def _conv_taps(lc, w_ref):
    return [jnp.broadcast_to(w_ref[lc, j:j + 1, :], (SUBLANES, LANES)) for j in range(CONV_WIDTH)]


def _conv_windows(offsets):
    windows = {}
    for g, off in enumerate(offsets):
        for j in range(CONV_WIDTH):
            windows.setdefault(off + j, []).append((g, j))
    return windows


def _conv_groups(full_ref, lc, base, offsets, taps):
    head = CONV_PAD - CONV_HIST
    accs = [jnp.zeros((SUBLANES, LANES), F32) for _ in offsets]
    for woff, uses in _conv_windows(offsets).items():
        xo = full_ref[lc, pl.ds(base + (head + woff), SUBLANES), :]
        for g, j in uses:
            accs[g] = accs[g] + xo * taps[j]
    return jnp.concatenate(accs, axis=0)


def _ln_silu(cv, b_ref, lng_ref, lnb_ref):
    cv = cv + b_ref[...]
    mu = jnp.mean(cv, axis=-1, keepdims=True)
    var = jnp.mean(jnp.square(cv - mu), axis=-1, keepdims=True)
    return jax.nn.silu((cv - mu) * lax.rsqrt(var + EPS) * lng_ref[...] + lnb_ref[...])


def _inproj_body(*refs, n_heads, d_ret, d_conv, fuse_conv, steps_per_seq, conv_rt):
    if fuse_conv:
        (x_ref, g_ref, w_ref, cos_ref, sin_ref, hist_ref, cw_ref, cb_ref, lng_ref, lnb_ref,
         q_ref, k_ref, v_ref, gate_ref, y_ref, nhist_ref, h_ref, full_ref, cv_ref) = refs
    else:
        x_ref, g_ref, w_ref, cos_ref, sin_ref, q_ref, k_ref, v_ref, gate_ref, u_ref = refs
    tm = x_ref.shape[0]
    n_lc = d_conv // LANES
    head = CONV_PAD - CONV_HIST
    dq = n_heads * RET_DK
    col_a = 2 * dq + 2 * d_ret

    def proj(h, col, width):
        return jnp.dot(h, w_ref[:, col:col + width], preferred_element_type=F32)

    def rotary(t):
        cos, sin = cos_ref[...], sin_ref[...]
        t1, t2 = t[:, :ROPE_HALF], t[:, ROPE_HALF:]
        return jnp.concatenate([t1 * cos - t2 * sin, t2 * cos + t1 * sin], axis=-1)

    def project(h, kind, blk):
        lo = blk * RET_DK
        if kind == "q":
            q_ref[:, lo:lo + RET_DK] = (rotary(proj(h, lo, RET_DK)) * (RET_DK ** -0.5)).astype(BF16)
        elif kind == "k":
            k_ref[:, lo:lo + RET_DK] = rotary(proj(h, dq + lo, RET_DK)).astype(BF16)
        elif kind == "v":
            v_ref[:, lo:lo + RET_DK] = proj(h, 2 * dq + lo, RET_DK).astype(BF16)
        else:
            gate_ref[:, lo:lo + RET_DK] = proj(h, 2 * dq + d_ret + lo, RET_DK).astype(BF16)

    sections = [(kind, blk) for kind, width in (("q", dq), ("k", dq), ("v", d_ret), ("g", d_ret))
                for blk in range(width // RET_DK)]

    if not fuse_conv:
        h = _rms(x_ref[...], g_ref[...]).astype(BF16)
        for kind, blk in sections:
            project(h, kind, blk)
        u_ref[...] = proj(h, col_a, d_conv) * jax.nn.sigmoid(proj(h, col_a + d_conv, d_conv))
        return

    t = pl.program_id(0) % steps_per_seq

    @pl.when(t == 0)
    def _():
        for lc in range(n_lc):
            lanes = slice(lc * LANES, (lc + 1) * LANES)
            full_ref[lc, 0:SUBLANES, :] = jnp.zeros((SUBLANES, LANES), F32)
            full_ref[lc, head:CONV_PAD, :] = hist_ref[:, lanes]

    @pl.when(t > 0)
    def _():
        for lc in range(n_lc):
            full_ref[lc, head:CONV_PAD, :] = full_ref[lc, tm + head:tm + CONV_PAD, :]

    h_ref[...] = _rms(x_ref[...], g_ref[...]).astype(BF16)

    lc_per_blk = RET_DK // LANES
    for c in range(d_conv // RET_DK):
        h = h_ref[...]
        lo = c * RET_DK
        u = proj(h, col_a + lo, RET_DK) * jax.nn.sigmoid(proj(h, col_a + d_conv + lo, RET_DK))
        for s in range(lc_per_blk):
            full_ref[c * lc_per_blk + s, CONV_PAD:CONV_PAD + tm, :] = u[:, s * LANES:(s + 1) * LANES]

    offsets = [r * SUBLANES for r in range(conv_rt // SUBLANES)]
    units = [(r, lc) for r in range(tm // conv_rt) for lc in range(n_lc)]
    per_block = -(-len(units) // len(sections))

    def layer_norm(r):
        rows = slice(r * conv_rt, (r + 1) * conv_rt)
        cv = jnp.concatenate([cv_ref[lc, rows, :] for lc in range(n_lc)], axis=-1)
        y_ref[rows, :] = _ln_silu(cv, cb_ref, lng_ref, lnb_ref).astype(y_ref.dtype)

    done_rows = 0
    normed = 0
    for i, (kind, blk) in enumerate(sections):
        project(h_ref[...], kind, blk)
        for r, lc in units[i * per_block:(i + 1) * per_block]:
            cv_ref[lc, r * conv_rt:(r + 1) * conv_rt, :] = _conv_groups(
                full_ref, lc, r * conv_rt, offsets, _conv_taps(lc, cw_ref))
        for r in range(normed, done_rows):
            layer_norm(r)
        normed = done_rows
        done_rows = min((i + 1) * per_block, len(units)) // n_lc
    for r in range(normed, tm // conv_rt):
        layer_norm(r)
    for lc in range(n_lc):
        nhist_ref[:, lc * LANES:(lc + 1) * LANES] = full_ref[lc, tm + head:tm + CONV_PAD, :]


def _inproj(x, g, w_in, cos, sin, *, n_heads, d_ret, d_conv, tm, conv=None):
    n, d = x.shape
    period = cos.shape[0] // tm
    dq = n_heads * RET_DK
    row = lambda i: (i, 0)
    in_specs = [
        pl.BlockSpec((tm, d), row),
        _resident((1, d)),
        _resident(w_in.shape),
        pl.BlockSpec((tm, ROPE_HALF), lambda i: (i % period, 0)),
        pl.BlockSpec((tm, ROPE_HALF), lambda i: (i % period, 0)),
    ]
    out_specs = [pl.BlockSpec((tm, dq), row), pl.BlockSpec((tm, dq), row),
                 pl.BlockSpec((tm, d_ret), row), pl.BlockSpec((tm, d_ret), row),
                 pl.BlockSpec((tm, d_conv), row)]
    out_shape = [jax.ShapeDtypeStruct((n, dq), BF16), jax.ShapeDtypeStruct((n, dq), BF16),
                 jax.ShapeDtypeStruct((n, d_ret), BF16), jax.ShapeDtypeStruct((n, d_ret), BF16)]
    args = [x, g, w_in, cos, sin]
    scratch = []
    steps_per_seq = conv_rt = None
    if conv is None:
        out_shape.append(jax.ShapeDtypeStruct((n, d_conv), F32))
    else:
        hist, layer, seq, cw, cb, lng, lnb, conv_rt = conv
        assert seq % tm == 0 and tm % conv_rt == 0
        steps_per_seq = seq // tm
        cw = cw.reshape(cw.shape[0], d_conv // LANES, LANES).transpose(1, 0, 2)
        in_specs += [pl.BlockSpec((None, None, CONV_HIST, d_conv), lambda i: (layer, i // steps_per_seq, 0, 0)),
                     _resident(cw.shape), _resident(cb.shape), _resident(lng.shape), _resident(lnb.shape)]
        args += [hist, cw, cb, lng, lnb]
        out_specs.append(pl.BlockSpec((None, None, CONV_HIST, d_conv), lambda i: (0, i // steps_per_seq, 0, 0)))
        out_shape += [jax.ShapeDtypeStruct((n, d_conv), BF16),
                      jax.ShapeDtypeStruct((1, hist.shape[1], CONV_HIST, d_conv), F32)]
        scratch = [pltpu.VMEM((tm, d), BF16),
                   pltpu.VMEM((d_conv // LANES, tm + CONV_PAD, LANES), F32),
                   pltpu.VMEM((d_conv // LANES, tm, LANES), F32)]
    return pl.pallas_call(
        functools.partial(_inproj_body, n_heads=n_heads, d_ret=d_ret, d_conv=d_conv,
                          fuse_conv=conv is not None, steps_per_seq=steps_per_seq, conv_rt=conv_rt),
        grid=(n // tm,),
        in_specs=in_specs,
        out_specs=out_specs,
        out_shape=out_shape,
        scratch_shapes=scratch,
        compiler_params=_params("arbitrary" if conv is not None else "parallel"),
        name="inproj",
    )(*args)


def _retention_body(q_ref, k_ref, v_ref, gate_ref, s0_ref, dmask_ref, qdec_ref, kdec_ref,
                    cdec_ref, gng_ref, gnb_ref, y_ref, so_ref, *, n_heads, nb, nc, c, steps):
    if steps == 1:
        state_ref = s0_ref
    else:
        state_ref = so_ref

        @pl.when(pl.program_id(1) == 0)
        def _():
            so_ref[...] = s0_ref[...]

    narrow = c % (2 * SUBLANES) != 0
    mm = F32 if narrow else BF16
    nt = (((1,), (1,)), ((), ()))
    tn = (((0,), (0,)), ((), ()))

    for hd in range(n_heads):
        cols = slice(hd * RET_DK, (hd + 1) * RET_DK)
        vcols = slice(hd * RET_DV, (hd + 1) * RET_DV)
        dmask = dmask_ref[hd]
        qdec = qdec_ref[hd]
        kdec = kdec_ref[hd]
        cdec = cdec_ref[hd]
        gng = gng_ref[:, vcols]
        gnb = gnb_ref[:, vcols]
        if narrow:
            q_all = q_ref[:, cols].astype(F32)
            k_all = k_ref[:, cols].astype(F32)
            v_all = v_ref[:, vcols].astype(F32)
            gate_all = gate_ref[:, vcols].astype(F32)
        for bb in range(nb):
            state = state_ref[bb, hd]
            for cc in range(nc):
                rows = slice((bb * nc + cc) * c, (bb * nc + cc + 1) * c)
                if narrow:
                    q, k, v, gate = q_all[rows], k_all[rows], v_all[rows], gate_all[rows]
                else:
                    q, k, v = q_ref[rows, cols], k_ref[rows, cols], v_ref[rows, vcols]
                    gate = gate_ref[rows, vcols].astype(F32)
                s = lax.dot_general(q, k, nt, preferred_element_type=F32) * dmask
                inner = jnp.dot(s.astype(mm), v, preferred_element_type=F32)
                qd = (q.astype(F32) * qdec).astype(mm)
                cross = jnp.dot(qd, state.astype(mm), preferred_element_type=F32)
                o = inner + cross
                kd = (k.astype(F32) * kdec).astype(mm)
                state = state * cdec + lax.dot_general(kd, v, tn, preferred_element_type=F32)
                mu = jnp.mean(o, axis=-1, keepdims=True)
                var = jnp.mean(jnp.square(o - mu), axis=-1, keepdims=True)
                on = (o - mu) * lax.rsqrt(var + EPS) * gng + gnb
                y_ref[rows, vcols] = (jax.nn.silu(gate) * on).astype(y_ref.dtype)
            so_ref[bb, hd] = state


def _retention(q, k, v, gate, state0, gng, gnb, *, layer, seq, c, nb, nc):
    n, d_ret = v.shape
    batch, n_heads = state0.shape[1:3]
    steps = seq // (nc * c)
    assert nb == 1 or steps == 1

    lg = jnp.log1p(-jnp.exp2(-5.0 - jnp.arange(n_heads, dtype=F32)))
    idx = jnp.arange(c, dtype=F32)
    diff = idx[:, None] - idx[None, :]
    causal = diff >= 0
    dmask = jnp.where(causal[None], jnp.exp(jnp.where(causal, diff, 0.0)[None] * lg[:, None, None]), 0.0)
    qdec = jnp.broadcast_to(jnp.exp((idx[None, :] + 1.0) * lg[:, None])[:, :, None], (n_heads, c, RET_DK))
    kdec = jnp.broadcast_to(jnp.exp((c - 1.0 - idx[None, :]) * lg[:, None])[:, :, None], (n_heads, c, RET_DK))
    cdec = jnp.broadcast_to(jnp.exp(c * lg)[:, None, None], (n_heads, 1, RET_DV))

    rows = nb * nc * c
    row = lambda b, t: (b * steps + t, 0)
    st_blk = (None, nb) + state0.shape[2:]
    y_dtype = BF16 if c % (2 * SUBLANES) == 0 else F32
    return pl.pallas_call(
        functools.partial(_retention_body, n_heads=n_heads, nb=nb, nc=nc, c=c, steps=steps),
        grid=(batch // nb, steps),
        in_specs=[
            pl.BlockSpec((rows, q.shape[1]), row),
            pl.BlockSpec((rows, k.shape[1]), row),
            pl.BlockSpec((rows, d_ret), row),
            pl.BlockSpec((rows, d_ret), row),
            pl.BlockSpec(st_blk, lambda b, t: (layer, b, 0, 0, 0)),
            _resident(dmask.shape),
            _resident(qdec.shape),
            _resident(kdec.shape),
            _resident(cdec.shape),
            _resident(gng.shape),
            _resident(gnb.shape),
        ],
        out_specs=[
            pl.BlockSpec((rows, d_ret), row),
            pl.BlockSpec(st_blk, lambda b, t: (0, b, 0, 0, 0)),
        ],
        out_shape=[
            jax.ShapeDtypeStruct((n, d_ret), y_dtype),
            jax.ShapeDtypeStruct((1,) + state0.shape[1:], state0.dtype),
        ],
        compiler_params=_params("parallel", "arbitrary"),
        name="retention",
    )(q, k, v, gate, state0, dmask, qdec, kdec, cdec, gng, gnb)


def _conv_body(u_ref, buf_ref, w_ref, b_ref, lng_ref, lnb_ref, y_ref, nbuf_ref, full_ref, cv_ref,
               *, bb, seq):
    d = u_ref.shape[-1]
    n_lc = d // LANES
    head = CONV_PAD - CONV_HIST
    seg = seq + CONV_PAD

    def conv_tile(offsets):
        def conv_lanes(lc, carry):
            cv_ref[lc] = _conv_groups(full_ref, lc, 0, offsets, _conv_taps(lc, w_ref))
            return carry

        lax.fori_loop(0, n_lc, conv_lanes, 0)
        cv = jnp.concatenate([cv_ref[lc] for lc in range(n_lc)], axis=-1)
        return _ln_silu(cv, b_ref, lng_ref, lnb_ref).astype(y_ref.dtype)

    for b in range(bb):
        for lc in range(n_lc):
            lanes = slice(lc * LANES, (lc + 1) * LANES)
            full_ref[lc, b * seg:b * seg + SUBLANES, :] = jnp.zeros((SUBLANES, LANES), F32)
            full_ref[lc, b * seg + head:b * seg + CONV_PAD, :] = buf_ref[b, :, lanes]
            full_ref[lc, b * seg + CONV_PAD:(b + 1) * seg, :] = u_ref[b, :, lanes]

    y_ref[...] = conv_tile([b * seg for b in range(bb)]).reshape(bb, seq, d)

    for b in range(bb):
        for lc in range(n_lc):
            lanes = slice(lc * LANES, (lc + 1) * LANES)
            nbuf_ref[b, :, lanes] = full_ref[lc, b * seg + seq + head:(b + 1) * seg, :]


def _conv(u, buf, w, b, lng, lnb, *, layer, bb):
    batch, seq, d = u.shape
    assert seq == SUBLANES
    w = w.reshape(w.shape[0], d // LANES, LANES).transpose(1, 0, 2)
    y_dtype = BF16 if seq % (2 * SUBLANES) == 0 else F32
    blk = lambda i: (i, 0, 0)
    hist = lambda i: (layer, i, 0, 0)
    return pl.pallas_call(
        functools.partial(_conv_body, bb=bb, seq=seq),
        grid=(batch // bb,),
        in_specs=[
            pl.BlockSpec((bb, seq, d), blk),
            pl.BlockSpec((None, bb, CONV_HIST, d), hist),
            _resident(w.shape),
            _resident(b.shape),
            _resident(lng.shape),
            _resident(lnb.shape),
        ],
        out_specs=[
            pl.BlockSpec((bb, seq, d), blk),
            pl.BlockSpec((None, bb, CONV_HIST, d), lambda i: (0, i, 0, 0)),
        ],
        out_shape=[
            jax.ShapeDtypeStruct((batch, seq, d), y_dtype),
            jax.ShapeDtypeStruct((1, batch, CONV_HIST, d), u.dtype),
        ],
        scratch_shapes=[pltpu.VMEM((d // LANES, bb * (seq + CONV_PAD), LANES), F32),
                        pltpu.VMEM((d // LANES, bb * seq, LANES), F32)],
        compiler_params=_params("parallel"),
        name="conv",
    )(u, buf, w, b, lng, lnb)


def _outproj_body(x_ref, r_ref, c_ref, w_ref, o_ref, *, d_ret):
    y = jnp.dot(r_ref[...].astype(BF16), w_ref[:d_ret, :], preferred_element_type=F32)
    y += jnp.dot(c_ref[...].astype(BF16), w_ref[d_ret:, :], preferred_element_type=F32)
    o_ref[...] = x_ref[...] + y


def _outproj(x, ret_y, conv_y, w_out, *, tm):
    n, d = x.shape
    d_ret, d_conv = ret_y.shape[1], conv_y.shape[1]
    row = lambda i: (i, 0)
    return pl.pallas_call(
        functools.partial(_outproj_body, d_ret=d_ret),
        grid=(n // tm,),
        in_specs=[
            pl.BlockSpec((tm, d), row),
            pl.BlockSpec((tm, d_ret), row),
            pl.BlockSpec((tm, d_conv), row),
            _resident(w_out.shape),
        ],
        out_specs=pl.BlockSpec((tm, d), row),
        out_shape=jax.ShapeDtypeStruct((n, d), F32),
        compiler_params=_params("parallel"),
        name="outproj",
    )(x, ret_y, conv_y, w_out)


def _rope_tables(pos, rows):
    inv_freq = ROPE_BASE ** (-jnp.arange(ROPE_HALF, dtype=F32) / ROPE_HALF)
    ang = pos[:, None] * inv_freq[None, :]
    reps = max(1, rows // pos.shape[0])
    return jnp.tile(jnp.cos(ang), (reps, 1)), jnp.tile(jnp.sin(ang), (reps, 1))


def _stream(x, pos, ret_state, conv_buf, wts, ffn1_w, ffn2_w, *, layer, emit_bf16, ffn_tm, tf, tm,
            ret_nb, ret_nc, conv_bb, conv_rt):
    batch, seq, d = x.shape
    n_heads = ret_state.shape[2]
    d_ret = n_heads * RET_DV
    d_conv = conv_buf.shape[-1]
    c = min(seq, RET_CHUNK)
    xf = x.reshape(batch * seq, d)

    x1 = _ffn(xf, wts["norm_ffn1_g"], *ffn1_w, wts["norm_final_g"], final_norm=False,
              tm=ffn_tm, tf=tf, emit_bf16=emit_bf16,
              cast_jobs=(wts["w_in"], wts["w_out"]) if emit_bf16 else ())
    if emit_bf16:
        x1, *ffn1_w = x1
        *ffn1_w, w_in, w_out = ffn1_w
        wts = dict(wts, w_in=w_in, w_out=w_out)
    cos, sin = _rope_tables(pos, tm)
    conv_w = (wts["conv_w"], wts["conv_b"], wts["conv_ln_g"], wts["conv_ln_b"])
    fuse_conv = seq % tm == 0
    q, k, v, gate, *conv_out = _inproj(
        x1, wts["norm_mix_g"], wts["w_in"], cos, sin, n_heads=n_heads, d_ret=d_ret, d_conv=d_conv, tm=tm,
        conv=(conv_buf, layer, seq, *conv_w, conv_rt) if fuse_conv else None)
    ret_y, new_ret = _retention(q, k, v, gate, ret_state, wts["ret_gn_g"], wts["ret_gn_b"],
                                layer=layer, seq=seq, c=c, nb=ret_nb, nc=ret_nc)
    if fuse_conv:
        conv_y, new_buf = conv_out
    else:
        conv_y, new_buf = _conv(conv_out[0].reshape(batch, seq, d_conv), conv_buf, *conv_w,
                                layer=layer, bb=conv_bb)
        conv_y = conv_y.reshape(batch * seq, d_conv)
    x2 = _outproj(x1, ret_y, conv_y, wts["w_out"], tm=tm)
    y = _ffn(x2, wts["norm_ffn2_g"], *ffn2_w, wts["norm_final_g"], final_norm=True,
             tm=ffn_tm, tf=tf, emit_bf16=emit_bf16)
    if emit_bf16:
        y, *ffn2_w = y
    return y.reshape(batch, seq, d), new_ret, new_buf, wts, tuple(ffn1_w), tuple(ffn2_w)


def kernel(x_prompt, x_sample, state_retention, state_conv, norm_ffn1_g, ffn1_w1, ffn1_w3, ffn1_w2, norm_mix_g, w_in, ret_gn_g, ret_gn_b, conv_w, conv_b, conv_ln_g, conv_ln_b, w_out, norm_ffn2_g, ffn2_w1, ffn2_w3, ffn2_w2, norm_final_g):
    depth = w_in.shape[0]
    assert depth == 1
    layer = 0
    row = lambda g: g.reshape(1, -1)
    wts = {
        "norm_ffn1_g": row(norm_ffn1_g[layer]),
        "norm_mix_g": row(norm_mix_g[layer]),
        "w_in": w_in[layer],
        "ret_gn_g": row(ret_gn_g[layer]),
        "ret_gn_b": row(ret_gn_b[layer]),
        "conv_w": conv_w[layer],
        "conv_b": row(conv_b[layer]),
        "conv_ln_g": row(conv_ln_g[layer]),
        "conv_ln_b": row(conv_ln_b[layer]),
        "w_out": w_out[layer],
        "norm_ffn2_g": row(norm_ffn2_g[layer]),
        "norm_final_g": row(norm_final_g),
    }
    bp, tp, _ = x_prompt.shape
    bs, ts, _ = x_sample.shape
    n_heads = state_retention.shape[2]
    d_conv = state_conv.shape[-1]

    pos_p = jnp.arange(tp, dtype=F32)
    pos_s = PAST_LEN + jnp.arange(ts, dtype=F32)
    r0 = jnp.zeros((1, bp, n_heads, RET_DK, RET_DV), state_retention.dtype)
    c0 = jnp.zeros((1, bp, CONV_HIST, d_conv), x_prompt.dtype)

    ffn1_w = (ffn1_w1[layer], ffn1_w3[layer], ffn1_w2[layer])
    ffn2_w = (ffn2_w1[layer], ffn2_w3[layer], ffn2_w2[layer])
    y_s, ret_s, conv_s, wts, ffn1_w, ffn2_w = _stream(
        x_sample, pos_s, state_retention, state_conv, wts, ffn1_w, ffn2_w, layer=layer,
        emit_bf16=True, ffn_tm=1024, tf=256, tm=512, ret_nb=8, ret_nc=1, conv_bb=16, conv_rt=None)
    y_p, ret_p, conv_p, _, _, _ = _stream(
        x_prompt, pos_p, r0, c0, wts, ffn1_w, ffn2_w, layer=0,
        emit_bf16=False, ffn_tm=1024, tf=512, tm=512, ret_nb=1, ret_nc=4, conv_bb=None, conv_rt=128)
    return (y_p, y_s, ret_p, conv_p, ret_s, conv_s)
```

```python
import functools

import jax
import jax.numpy as jnp
from jax import lax
from jax.experimental import pallas as pl
from jax.experimental.pallas import tpu as pltpu

F32 = jnp.float32
BF16 = jnp.bfloat16

RET_DK = 256
RET_DV = 256
RET_CHUNK = 128
CONV_WIDTH = 31
ROPE_BASE = 10000.0
EPS = 1e-6
PAST_LEN = 16384

LANES = 128
SUBLANES = 8
VMEM_LIMIT_BYTES = 60 * 1024 * 1024

ROPE_HALF = RET_DK // 2
CONV_HIST = CONV_WIDTH - 1
CONV_PAD = 32
ROW_CHUNK = 128
CAST_ROWS = 128


def _rms(x, g):
    return x * lax.rsqrt(jnp.mean(x * x, axis=-1, keepdims=True) + EPS) * g


def _params(*semantics):
    return pltpu.CompilerParams(dimension_semantics=semantics,
                                vmem_limit_bytes=VMEM_LIMIT_BYTES)


def _resident(shape):
    nd = len(shape)
    return pl.BlockSpec(shape, lambda *_: (0,) * nd, pipeline_mode=pl.Buffered(1))


def _ffn_body(x_ref, g_ref, w1_ref, w3_ref, w2_ref, gf_ref, *rest, final_norm, emit_bf16, n_cast):
    n_emit = 3 if emit_bf16 else 0
    cast_src = rest[:n_cast]
    o_ref = rest[n_cast]
    emit_refs = rest[n_cast + 1:n_cast + 1 + n_emit]
    cast_dst = rest[n_cast + 1 + n_emit:n_cast + 1 + n_emit + n_cast]
    h_ref = rest[-1]
    j = pl.program_id(1)

    def step(first, last):
        if first:
            h = _rms(x_ref[...], g_ref[...]).astype(BF16)
            h_ref[...] = h
        else:
            h = h_ref[...]
        w1, w3, w2 = (w_ref[...].astype(BF16) for w_ref in (w1_ref, w3_ref, w2_ref))
        for dst_ref, w in zip(emit_refs, (w1, w3, w2)):
            dst_ref[...] = w
        for src_ref, dst_ref in zip(cast_src, cast_dst):
            dst_ref[...] = src_ref[...].astype(BF16)
        a = jnp.dot(h, w1, preferred_element_type=F32)
        b = jnp.dot(h, w3, preferred_element_type=F32)
        act = (jax.nn.silu(a) * b).astype(BF16)
        part = jnp.dot(act, w2, preferred_element_type=F32)
        if first:
            o_ref[...] = part
        elif last and not final_norm:
            o_ref[...] = x_ref[...] + 0.5 * (o_ref[...] + part)
        else:
            o_ref[...] += part

    n_steps = pl.num_programs(1)

    @pl.when(j == 0)
    def _():
        step(True, False)

    @pl.when(jnp.logical_and(j > 0, j < n_steps - 1))
    def _():
        step(False, False)

    @pl.when(j == n_steps - 1)
    def _():
        step(False, True)
        if final_norm:
            def finish(r, carry):
                rows = pl.ds(pl.multiple_of(r * ROW_CHUNK, ROW_CHUNK), ROW_CHUNK)
                o_ref[rows, :] = _rms(x_ref[rows, :] + 0.5 * o_ref[rows, :], gf_ref[...])
                return carry
            lax.fori_loop(0, x_ref.shape[0] // ROW_CHUNK, finish, 0)


def _ffn(x, g, w1, w3, w2, gf, *, final_norm, tm, tf, emit_bf16=False, cast_jobs=()):
    n, d = x.shape
    f = w1.shape[1]
    steps = f // tf
    assert steps >= 2
    up = pl.BlockSpec((d, tf), lambda i, j: (0, j))
    down = pl.BlockSpec((tf, d), lambda i, j: (j, 0))
    out_specs = [pl.BlockSpec((tm, d), lambda i, j: (i, 0))]
    out_shape = [jax.ShapeDtypeStruct((n, d), F32)]
    if emit_bf16:
        out_specs += [up, up, down]
        out_shape += [jax.ShapeDtypeStruct(w.shape, BF16) for w in (w1, w3, w2)]
    cast_specs = []
    for m in cast_jobs:
        blocks = min(steps, m.shape[0] // CAST_ROWS)
        rows = m.shape[0] // blocks
        assert rows * blocks == m.shape[0] and rows % (2 * SUBLANES) == 0
        cast_specs.append(pl.BlockSpec((rows, m.shape[1]), lambda i, j, nb=blocks: (jnp.minimum(j, nb - 1), 0)))
        out_shape.append(jax.ShapeDtypeStruct(m.shape, BF16))
    outs = pl.pallas_call(
        functools.partial(_ffn_body, final_norm=final_norm, emit_bf16=emit_bf16, n_cast=len(cast_jobs)),
        grid=(n // tm, steps),
        in_specs=[
            pl.BlockSpec((tm, d), lambda i, j: (i, 0)),
            pl.BlockSpec((1, d), lambda i, j: (0, 0)),
            up,
            up,
            down,
            pl.BlockSpec((1, d), lambda i, j: (0, 0)),
        ] + cast_specs,
        out_specs=out_specs + cast_specs,
        out_shape=out_shape,
        scratch_shapes=[pltpu.VMEM((tm, d), BF16)],
        compiler_params=_params("parallel", "arbitrary"),
        name="ffn",
    )(x, g, w1, w3, w2, gf, *cast_jobs)
    return outs if len(outs) > 1 else outs[0]


def _conv_taps(lc, w_ref):
    return [jnp.broadcast_to(w_ref[lc, j:j + 1, :], (SUBLANES, LANES)) for j in range(CONV_WIDTH)]


def _conv_windows(offsets):
    windows = {}
    for g, off in enumerate(offsets):
        for j in range(CONV_WIDTH):
            windows.setdefault(off + j, []).append((g, j))
    return windows


def _conv_groups(full_ref, lc, base, offsets, taps):
    head = CONV_PAD - CONV_HIST
    accs = [jnp.zeros((SUBLANES, LANES), F32) for _ in offsets]
    for woff, uses in _conv_windows(offsets).items():
        xo = full_ref[lc, pl.ds(base + (head + woff), SUBLANES), :]
        for g, j in uses:
            accs[g] = accs[g] + xo * taps[j]
    return jnp.concatenate(accs, axis=0)


def _ln_silu(cv, b_ref, lng_ref, lnb_ref):
    cv = cv + b_ref[...]
    mu = jnp.mean(cv, axis=-1, keepdims=True)
    var = jnp.mean(jnp.square(cv - mu), axis=-1, keepdims=True)
    return jax.nn.silu((cv - mu) * lax.rsqrt(var + EPS) * lng_ref[...] + lnb_ref[...])


def _inproj_body(*refs, n_heads, d_ret, d_conv, fuse_conv, steps_per_seq, conv_rt):
    if fuse_conv:
        (x_ref, g_ref, w_ref, cos_ref, sin_ref, hist_ref, cw_ref, cb_ref, lng_ref, lnb_ref,
         q_ref, k_ref, v_ref, gate_ref, y_ref, nhist_ref, h_ref, full_ref, cv_ref) = refs
    else:
        x_ref, g_ref, w_ref, cos_ref, sin_ref, q_ref, k_ref, v_ref, gate_ref, u_ref = refs
    tm = x_ref.shape[0]
    n_lc = d_conv // LANES
    head = CONV_PAD - CONV_HIST
    dq = n_heads * RET_DK
    col_a = 2 * dq + 2 * d_ret

    def proj(h, col, width):
        return jnp.dot(h, w_ref[:, col:col + width], preferred_element_type=F32)

    def rotary(t):
        cos, sin = cos_ref[...], sin_ref[...]
        t1, t2 = t[:, :ROPE_HALF], t[:, ROPE_HALF:]
        return jnp.concatenate([t1 * cos - t2 * sin, t2 * cos + t1 * sin], axis=-1)

    def project(h, kind, blk):
        lo = blk * RET_DK
        if kind == "q":
            q_ref[:, lo:lo + RET_DK] = (rotary(proj(h, lo, RET_DK)) * (RET_DK ** -0.5)).astype(BF16)
        elif kind == "k":
            k_ref[:, lo:lo + RET_DK] = rotary(proj(h, dq + lo, RET_DK)).astype(BF16)
        elif kind == "v":
            v_ref[:, lo:lo + RET_DK] = proj(h, 2 * dq + lo, RET_DK).astype(BF16)
        else:
            gate_ref[:, lo:lo + RET_DK] = proj(h, 2 * dq + d_ret + lo, RET_DK).astype(BF16)

    sections = [(kind, blk) for kind, width in (("q", dq), ("k", dq), ("v", d_ret), ("g", d_ret))
                for blk in range(width // RET_DK)]

    if not fuse_conv:
        h = _rms(x_ref[...], g_ref[...]).astype(BF16)
        for kind, blk in sections:
            project(h, kind, blk)
        u_ref[...] = proj(h, col_a, d_conv) * jax.nn.sigmoid(proj(h, col_a + d_conv, d_conv))
        return

    t = pl.program_id(0) % steps_per_seq

    @pl.when(t == 0)
    def _():
        for lc in range(n_lc):
            lanes = slice(lc * LANES, (lc + 1) * LANES)
            full_ref[lc, 0:SUBLANES, :] = jnp.zeros((SUBLANES, LANES), F32)
            full_ref[lc, head:CONV_PAD, :] = hist_ref[:, lanes]

    @pl.when(t > 0)
    def _():
        for lc in range(n_lc):
            full_ref[lc, head:CONV_PAD, :] = full_ref[lc, tm + head:tm + CONV_PAD, :]

    h_ref[...] = _rms(x_ref[...], g_ref[...]).astype(BF16)

    lc_per_blk = RET_DK // LANES
    for c in range(d_conv // RET_DK):
        h = h_ref[...]
        lo = c * RET_DK
        u = proj(h, col_a + lo, RET_DK) * jax.nn.sigmoid(proj(h, col_a + d_conv + lo, RET_DK))
        for s in range(lc_per_blk):
            full_ref[c * lc_per_blk + s, CONV_PAD:CONV_PAD + tm, :] = u[:, s * LANES:(s + 1) * LANES]

    offsets = [r * SUBLANES for r in range(conv_rt // SUBLANES)]
    units = [(r, lc) for r in range(tm // conv_rt) for lc in range(n_lc)]
    per_block = -(-len(units) // len(sections))

    def layer_norm(r):
        rows = slice(r * conv_rt, (r + 1) * conv_rt)
        cv = jnp.concatenate([cv_ref[lc, rows, :] for lc in range(n_lc)], axis=-1)
        y_ref[rows, :] = _ln_silu(cv, cb_ref, lng_ref, lnb_ref).astype(y_ref.dtype)

    done_rows = 0
    normed = 0
    for i, (kind, blk) in enumerate(sections):
        project(h_ref[...], kind, blk)
        for r, lc in units[i * per_block:(i + 1) * per_block]:
            cv_ref[lc, r * conv_rt:(r + 1) * conv_rt, :] = _conv_groups(
                full_ref, lc, r * conv_rt, offsets, _conv_taps(lc, cw_ref))
        for r in range(normed, done_rows):
            layer_norm(r)
        normed = done_rows
        done_rows = min((i + 1) * per_block, len(units)) // n_lc
    for r in range(normed, tm // conv_rt):
        layer_norm(r)
    for lc in range(n_lc):
        nhist_ref[:, lc * LANES:(lc + 1) * LANES] = full_ref[lc, tm + head:tm + CONV_PAD, :]


def _inproj(x, g, w_in, cos, sin, *, n_heads, d_ret, d_conv, tm, conv=None):
    n, d = x.shape
    period = cos.shape[0] // tm
    dq = n_heads * RET_DK
    row = lambda i: (i, 0)
    in_specs = [
        pl.BlockSpec((tm, d), row),
        _resident((1, d)),
        _resident(w_in.shape),
        pl.BlockSpec((tm, ROPE_HALF), lambda i: (i % period, 0)),
        pl.BlockSpec((tm, ROPE_HALF), lambda i: (i % period, 0)),
    ]
    out_specs = [pl.BlockSpec((tm, dq), row), pl.BlockSpec((tm, dq), row),
                 pl.BlockSpec((tm, d_ret), row), pl.BlockSpec((tm, d_ret), row),
                 pl.BlockSpec((tm, d_conv), row)]
    out_shape = [jax.ShapeDtypeStruct((n, dq), BF16), jax.ShapeDtypeStruct((n, dq), BF16),
                 jax.ShapeDtypeStruct((n, d_ret), BF16), jax.ShapeDtypeStruct((n, d_ret), BF16)]
    args = [x, g, w_in, cos, sin]
    scratch = []
    steps_per_seq = conv_rt = None
    if conv is None:
        out_shape.append(jax.ShapeDtypeStruct((n, d_conv), F32))
    else:
        hist, layer, seq, cw, cb, lng, lnb, conv_rt = conv
        assert seq % tm == 0 and tm % conv_rt == 0
        steps_per_seq = seq // tm
        cw = cw.reshape(cw.shape[0], d_conv // LANES, LANES).transpose(1, 0, 2)
        in_specs += [pl.BlockSpec((None, None, CONV_HIST, d_conv), lambda i: (layer, i // steps_per_seq, 0, 0)),
                     _resident(cw.shape), _resident(cb.shape), _resident(lng.shape), _resident(lnb.shape)]
        args += [hist, cw, cb, lng, lnb]
        out_specs.append(pl.BlockSpec((None, None, CONV_HIST, d_conv), lambda i: (0, i // steps_per_seq, 0, 0)))
        out_shape += [jax.ShapeDtypeStruct((n, d_conv), BF16),
                      jax.ShapeDtypeStruct((1, hist.shape[1], CONV_HIST, d_conv), F32)]
        scratch = [pltpu.VMEM((tm, d), BF16),
                   pltpu.VMEM((d_conv // LANES, tm + CONV_PAD, LANES), F32),
                   pltpu.VMEM((d_conv // LANES, tm, LANES), F32)]
    return pl.pallas_call(
        functools.partial(_inproj_body, n_heads=n_heads, d_ret=d_ret, d_conv=d_conv,
                          fuse_conv=conv is not None, steps_per_seq=steps_per_seq, conv_rt=conv_rt),
        grid=(n // tm,),
        in_specs=in_specs,
        out_specs=out_specs,
        out_shape=out_shape,
        scratch_shapes=scratch,
        compiler_params=_params("arbitrary" if conv is not None else "parallel"),
        name="inproj",
    )(*args)


def _retention_body(q_ref, k_ref, v_ref, gate_ref, s0_ref, dmask_ref, qdec_ref, kdec_ref,
                    cdec_ref, gng_ref, gnb_ref, y_ref, so_ref, *, n_heads, nb, nc, c, steps):
    if steps == 1:
        state_ref = s0_ref
    else:
        state_ref = so_ref

        @pl.when(pl.program_id(1) == 0)
        def _():
            so_ref[...] = s0_ref[...]

    narrow = c % (2 * SUBLANES) != 0
    mm = F32 if narrow else BF16
    nt = (((1,), (1,)), ((), ()))
    tn = (((0,), (0,)), ((), ()))

    for hd in range(n_heads):
        cols = slice(hd * RET_DK, (hd + 1) * RET_DK)
        vcols = slice(hd * RET_DV, (hd + 1) * RET_DV)
        dmask = dmask_ref[hd]
        qdec = qdec_ref[hd]
        kdec = kdec_ref[hd]
        cdec = cdec_ref[hd]
        gng = gng_ref[:, vcols]
        gnb = gnb_ref[:, vcols]
        if narrow:
            q_all = q_ref[:, cols].astype(F32)
            k_all = k_ref[:, cols].astype(F32)
            v_all = v_ref[:, vcols].astype(F32)
            gate_all = gate_ref[:, vcols].astype(F32)
        for bb in range(nb):
            state = state_ref[bb, hd]
            for cc in range(nc):
                rows = slice((bb * nc + cc) * c, (bb * nc + cc + 1) * c)
                if narrow:
                    q, k, v, gate = q_all[rows], k_all[rows], v_all[rows], gate_all[rows]
                else:
                    q, k, v = q_ref[rows, cols], k_ref[rows, cols], v_ref[rows, vcols]
                    gate = gate_ref[rows, vcols].astype(F32)
                s = lax.dot_general(q, k, nt, preferred_element_type=F32) * dmask
                inner = jnp.dot(s.astype(mm), v, preferred_element_type=F32)
                qd = (q.astype(F32) * qdec).astype(mm)
                cross = jnp.dot(qd, state.astype(mm), preferred_element_type=F32)
                o = inner + cross
                kd = (k.astype(F32) * kdec).astype(mm)
                state = state * cdec + lax.dot_general(kd, v, tn, preferred_element_type=F32)
                mu = jnp.mean(o, axis=-1, keepdims=True)
                var = jnp.mean(jnp.square(o - mu), axis=-1, keepdims=True)
                on = (o - mu) * lax.rsqrt(var + EPS) * gng + gnb
                y_ref[rows, vcols] = (jax.nn.silu(gate) * on).astype(y_ref.dtype)
            so_ref[bb, hd] = state


def _retention(q, k, v, gate, state0, gng, gnb, *, layer, seq, c, nb, nc):
    n, d_ret = v.shape
    batch, n_heads = state0.shape[1:3]
    steps = seq // (nc * c)
    assert nb == 1 or steps == 1

    lg = jnp.log1p(-jnp.exp2(-5.0 - jnp.arange(n_heads, dtype=F32)))
    idx = jnp.arange(c, dtype=F32)
    diff = idx[:, None] - idx[None, :]
    causal = diff >= 0
    dmask = jnp.where(causal[None], jnp.exp(jnp.where(causal, diff, 0.0)[None] * lg[:, None, None]), 0.0)
    qdec = jnp.broadcast_to(jnp.exp((idx[None, :] + 1.0) * lg[:, None])[:, :, None], (n_heads, c, RET_DK))
    kdec = jnp.broadcast_to(jnp.exp((c - 1.0 - idx[None, :]) * lg[:, None])[:, :, None], (n_heads, c, RET_DK))
    cdec = jnp.broadcast_to(jnp.exp(c * lg)[:, None, None], (n_heads, 1, RET_DV))

    rows = nb * nc * c
    row = lambda b, t: (b * steps + t, 0)
    st_blk = (None, nb) + state0.shape[2:]
    y_dtype = BF16 if c % (2 * SUBLANES) == 0 else F32
    return pl.pallas_call(
        functools.partial(_retention_body, n_heads=n_heads, nb=nb, nc=nc, c=c, steps=steps),
        grid=(batch // nb, steps),
        in_specs=[
            pl.BlockSpec((rows, q.shape[1]), row),
            pl.BlockSpec((rows, k.shape[1]), row),
            pl.BlockSpec((rows, d_ret), row),
            pl.BlockSpec((rows, d_ret), row),
            pl.BlockSpec(st_blk, lambda b, t: (layer, b, 0, 0, 0)),
            _resident(dmask.shape),
            _resident(qdec.shape),
            _resident(kdec.shape),
            _resident(cdec.shape),
            _resident(gng.shape),
            _resident(gnb.shape),
        ],
        out_specs=[
            pl.BlockSpec((rows, d_ret), row),
            pl.BlockSpec(st_blk, lambda b, t: (0, b, 0, 0, 0)),
        ],
        out_shape=[
            jax.ShapeDtypeStruct((n, d_ret), y_dtype),
            jax.ShapeDtypeStruct((1,) + state0.shape[1:], state0.dtype),
        ],
        compiler_params=_params("parallel", "arbitrary"),
        name="retention",
    )(q, k, v, gate, state0, dmask, qdec, kdec, cdec, gng, gnb)


def _conv_body(u_ref, buf_ref, w_ref, b_ref, lng_ref, lnb_ref, y_ref, nbuf_ref, full_ref, cv_ref,
               *, bb, seq):
    d = u_ref.shape[-1]
    n_lc = d // LANES
    head = CONV_PAD - CONV_HIST
    seg = seq + CONV_PAD

    def conv_tile(offsets):
        def conv_lanes(lc, carry):
            cv_ref[lc] = _conv_groups(full_ref, lc, 0, offsets, _conv_taps(lc, w_ref))
            return carry

        lax.fori_loop(0, n_lc, conv_lanes, 0)
        cv = jnp.concatenate([cv_ref[lc] for lc in range(n_lc)], axis=-1)
        return _ln_silu(cv, b_ref, lng_ref, lnb_ref).astype(y_ref.dtype)

    for b in range(bb):
        for lc in range(n_lc):
            lanes = slice(lc * LANES, (lc + 1) * LANES)
            full_ref[lc, b * seg:b * seg + SUBLANES, :] = jnp.zeros((SUBLANES, LANES), F32)
            full_ref[lc, b * seg + head:b * seg + CONV_PAD, :] = buf_ref[b, :, lanes]
            full_ref[lc, b * seg + CONV_PAD:(b + 1) * seg, :] = u_ref[b, :, lanes]

    y_ref[...] = conv_tile([b * seg for b in range(bb)]).reshape(bb, seq, d)

    for b in range(bb):
        for lc in range(n_lc):
            lanes = slice(lc * LANES, (lc + 1) * LANES)
            nbuf_ref[b, :, lanes] = full_ref[lc, b * seg + seq + head:(b + 1) * seg, :]


def _conv(u, buf, w, b, lng, lnb, *, layer, bb):
    batch, seq, d = u.shape
    assert seq == SUBLANES
    w = w.reshape(w.shape[0], d // LANES, LANES).transpose(1, 0, 2)
    y_dtype = BF16 if seq % (2 * SUBLANES) == 0 else F32
    blk = lambda i: (i, 0, 0)
    hist = lambda i: (layer, i, 0, 0)
    return pl.pallas_call(
        functools.partial(_conv_body, bb=bb, seq=seq),
        grid=(batch // bb,),
        in_specs=[
            pl.BlockSpec((bb, seq, d), blk),
            pl.BlockSpec((None, bb, CONV_HIST, d), hist),
            _resident(w.shape),
            _resident(b.shape),
            _resident(lng.shape),
            _resident(lnb.shape),
        ],
        out_specs=[
            pl.BlockSpec((bb, seq, d), blk),
            pl.BlockSpec((None, bb, CONV_HIST, d), lambda i: (0, i, 0, 0)),
        ],
        out_shape=[
            jax.ShapeDtypeStruct((batch, seq, d), y_dtype),
            jax.ShapeDtypeStruct((1, batch, CONV_HIST, d), u.dtype),
        ],
        scratch_shapes=[pltpu.VMEM((d // LANES, bb * (seq + CONV_PAD), LANES), F32),
                        pltpu.VMEM((d // LANES, bb * seq, LANES), F32)],
        compiler_params=_params("parallel"),
        name="conv",
    )(u, buf, w, b, lng, lnb)


def _outproj_body(x_ref, r_ref, c_ref, w_ref, o_ref, *, d_ret):
    y = jnp.dot(r_ref[...].astype(BF16), w_ref[:d_ret, :], preferred_element_type=F32)
    y += jnp.dot(c_ref[...].astype(BF16), w_ref[d_ret:, :], preferred_element_type=F32)
    o_ref[...] = x_ref[...] + y


def _outproj(x, ret_y, conv_y, w_out, *, tm):
    n, d = x.shape
    d_ret, d_conv = ret_y.shape[1], conv_y.shape[1]
    row = lambda i: (i, 0)
    return pl.pallas_call(
        functools.partial(_outproj_body, d_ret=d_ret),
        grid=(n // tm,),
        in_specs=[
            pl.BlockSpec((tm, d), row),
            pl.BlockSpec((tm, d_ret), row),
            pl.BlockSpec((tm, d_conv), row),
            _resident(w_out.shape),
        ],
        out_specs=pl.BlockSpec((tm, d), row),
        out_shape=jax.ShapeDtypeStruct((n, d), F32),
        compiler_params=_params("parallel"),
        name="outproj",
    )(x, ret_y, conv_y, w_out)


def _rope_tables(pos, rows):
    inv_freq = ROPE_BASE ** (-jnp.arange(ROPE_HALF, dtype=F32) / ROPE_HALF)
    ang = pos[:, None] * inv_freq[None, :]
    reps = max(1, rows // pos.shape[0])
    return jnp.tile(jnp.cos(ang), (reps, 1)), jnp.tile(jnp.sin(ang), (reps, 1))


def _stream(x, pos, ret_state, conv_buf, wts, ffn1_w, ffn2_w, *, layer, emit_bf16, ffn_tm, tf, tm,
            ret_nb, ret_nc, conv_bb, conv_rt):
    batch, seq, d = x.shape
    n_heads = ret_state.shape[2]
    d_ret = n_heads * RET_DV
    d_conv = conv_buf.shape[-1]
    c = min(seq, RET_CHUNK)
    xf = x.reshape(batch * seq, d)

    x1 = _ffn(xf, wts["norm_ffn1_g"], *ffn1_w, wts["norm_final_g"], final_norm=False,
              tm=ffn_tm, tf=tf, emit_bf16=emit_bf16,
              cast_jobs=(wts["w_in"], wts["w_out"]) if emit_bf16 else ())
    if emit_bf16:
        x1, *ffn1_w = x1
        *ffn1_w, w_in, w_out = ffn1_w
        wts = dict(wts, w_in=w_in, w_out=w_out)
    cos, sin = _rope_tables(pos, tm)
    conv_w = (wts["conv_w"], wts["conv_b"], wts["conv_ln_g"], wts["conv_ln_b"])
    fuse_conv = seq % tm == 0
    q, k, v, gate, *conv_out = _inproj(
        x1, wts["norm_mix_g"], wts["w_in"], cos, sin, n_heads=n_heads, d_ret=d_ret, d_conv=d_conv, tm=tm,
        conv=(conv_buf, layer, seq, *conv_w, conv_rt) if fuse_conv else None)
    ret_y, new_ret = _retention(q, k, v, gate, ret_state, wts["ret_gn_g"], wts["ret_gn_b"],
                                layer=layer, seq=seq, c=c, nb=ret_nb, nc=ret_nc)
    if fuse_conv:
        conv_y, new_buf = conv_out
    else:
        conv_y, new_buf = _conv(conv_out[0].reshape(batch, seq, d_conv), conv_buf, *conv_w,
                                layer=layer, bb=conv_bb)
        conv_y = conv_y.reshape(batch * seq, d_conv)
    x2 = _outproj(x1, ret_y, conv_y, wts["w_out"], tm=tm)
    y = _ffn(x2, wts["norm_ffn2_g"], *ffn2_w, wts["norm_final_g"], final_norm=True,
             tm=ffn_tm, tf=tf, emit_bf16=emit_bf16)
    if emit_bf16:
        y, *ffn2_w = y
    return y.reshape(batch, seq, d), new_ret, new_buf, wts, tuple(ffn1_w), tuple(ffn2_w)


def kernel(x_prompt, x_sample, state_retention, state_conv, norm_ffn1_g, ffn1_w1, ffn1_w3, ffn1_w2, norm_mix_g, w_in, ret_gn_g, ret_gn_b, conv_w, conv_b, conv_ln_g, conv_ln_b, w_out, norm_ffn2_g, ffn2_w1, ffn2_w3, ffn2_w2, norm_final_g):
    depth = w_in.shape[0]
    assert depth == 1
    layer = 0
    row = lambda g: g.reshape(1, -1)
    wts = {
        "norm_ffn1_g": row(norm_ffn1_g[layer]),
        "norm_mix_g": row(norm_mix_g[layer]),
        "w_in": w_in[layer],
        "ret_gn_g": row(ret_gn_g[layer]),
        "ret_gn_b": row(ret_gn_b[layer]),
        "conv_w": conv_w[layer],
        "conv_b": row(conv_b[layer]),
        "conv_ln_g": row(conv_ln_g[layer]),
        "conv_ln_b": row(conv_ln_b[layer]),
        "w_out": w_out[layer],
        "norm_ffn2_g": row(norm_ffn2_g[layer]),
        "norm_final_g": row(norm_final_g),
    }
    bp, tp, _ = x_prompt.shape
    bs, ts, _ = x_sample.shape
    n_heads = state_retention.shape[2]
    d_conv = state_conv.shape[-1]

    pos_p = jnp.arange(tp, dtype=F32)
    pos_s = PAST_LEN + jnp.arange(ts, dtype=F32)
    r0 = jnp.zeros((1, bp, n_heads, RET_DK, RET_DV), state_retention.dtype)
    c0 = jnp.zeros((1, bp, CONV_HIST, d_conv), x_prompt.dtype)

    ffn1_w = (ffn1_w1[layer], ffn1_w3[layer], ffn1_w2[layer])
    ffn2_w = (ffn2_w1[layer], ffn2_w3[layer], ffn2_w2[layer])
    y_s, ret_s, conv_s, wts, ffn1_w, ffn2_w = _stream(
        x_sample, pos_s, state_retention, state_conv, wts, ffn1_w, ffn2_w, layer=layer,
        emit_bf16=True, ffn_tm=1024, tf=256, tm=512, ret_nb=8, ret_nc=1, conv_bb=16, conv_rt=None)
    y_p, ret_p, conv_p, _, _, _ = _stream(
        x_prompt, pos_p, r0, c0, wts, ffn1_w, ffn2_w, layer=0,
        emit_bf16=False, ffn_tm=1024, tf=512, tm=512, ret_nb=1, ret_nc=8, conv_bb=None, conv_rt=128)
    return (y_p, y_s, ret_p, conv_p, ret_s, conv_s)
```

```python
import functools

import jax
import jax.numpy as jnp
from jax import lax
from jax.experimental import pallas as pl
from jax.experimental.pallas import tpu as pltpu

F32 = jnp.float32
BF16 = jnp.bfloat16

RET_DK = 256
RET_DV = 256
RET_CHUNK = 128
CONV_WIDTH = 31
ROPE_BASE = 10000.0
EPS = 1e-6
PAST_LEN = 16384

LANES = 128
SUBLANES = 8
VMEM_LIMIT_BYTES = 60 * 1024 * 1024

ROPE_HALF = RET_DK // 2
CONV_HIST = CONV_WIDTH - 1
CONV_PAD = 32
ROW_CHUNK = 128
CAST_ROWS = 128


def _rms(x, g):
    return x * lax.rsqrt(jnp.mean(x * x, axis=-1, keepdims=True) + EPS) * g


def _params(*semantics):
    return pltpu.CompilerParams(dimension_semantics=semantics,
                                vmem_limit_bytes=VMEM_LIMIT_BYTES)


def _resident(shape):
    nd = len(shape)
    return pl.BlockSpec(shape, lambda *_: (0,) * nd, pipeline_mode=pl.Buffered(1))


def _ffn_body(x_ref, g_ref, w1_ref, w3_ref, w2_ref, gf_ref, *rest, final_norm, emit_bf16, n_cast):
    n_emit = 3 if emit_bf16 else 0
    cast_src = rest[:n_cast]
    o_ref = rest[n_cast]
    emit_refs = rest[n_cast + 1:n_cast + 1 + n_emit]
    cast_dst = rest[n_cast + 1 + n_emit:n_cast + 1 + n_emit + n_cast]
    h_ref = rest[-1]
    j = pl.program_id(1)

    def step(first, last):
        if first:
            h = _rms(x_ref[...], g_ref[...]).astype(BF16)
            h_ref[...] = h
        else:
            h = h_ref[...]
        w1, w3, w2 = (w_ref[...].astype(BF16) for w_ref in (w1_ref, w3_ref, w2_ref))
        for dst_ref, w in zip(emit_refs, (w1, w3, w2)):
            dst_ref[...] = w
        for src_ref, dst_ref in zip(cast_src, cast_dst):
            dst_ref[...] = src_ref[...].astype(BF16)
        a = jnp.dot(h, w1, preferred_element_type=F32)
        b = jnp.dot(h, w3, preferred_element_type=F32)
        act = (jax.nn.silu(a) * b).astype(BF16)
        part = jnp.dot(act, w2, preferred_element_type=F32)
        if first:
            o_ref[...] = part
        elif last and not final_norm:
            o_ref[...] = x_ref[...] + 0.5 * (o_ref[...] + part)
        else:
            o_ref[...] += part

    n_steps = pl.num_programs(1)

    @pl.when(j == 0)
    def _():
        step(True, False)

    @pl.when(jnp.logical_and(j > 0, j < n_steps - 1))
    def _():
        step(False, False)

    @pl.when(j == n_steps - 1)
    def _():
        step(False, True)
        if final_norm:
            def finish(r, carry):
                rows = pl.ds(pl.multiple_of(r * ROW_CHUNK, ROW_CHUNK), ROW_CHUNK)
                o_ref[rows, :] = _rms(x_ref[rows, :] + 0.5 * o_ref[rows, :], gf_ref[...])
                return carry
            lax.fori_loop(0, x_ref.shape[0] // ROW_CHUNK, finish, 0)


def _ffn(x, g, w1, w3, w2, gf, *, final_norm, tm, tf, emit_bf16=False, cast_jobs=()):
    n, d = x.shape
    f = w1.shape[1]
    steps = f // tf
    assert steps >= 2
    up = pl.BlockSpec((d, tf), lambda i, j: (0, j))
    down = pl.BlockSpec((tf, d), lambda i, j: (j, 0))
    out_specs = [pl.BlockSpec((tm, d), lambda i, j: (i, 0))]
    out_shape = [jax.ShapeDtypeStruct((n, d), F32)]
    if emit_bf16:
        out_specs += [up, up, down]
        out_shape += [jax.ShapeDtypeStruct(w.shape, BF16) for w in (w1, w3, w2)]
    cast_specs = []
    for m in cast_jobs:
        blocks = min(steps, m.shape[0] // CAST_ROWS)
        rows = m.shape[0] // blocks
        assert rows * blocks == m.shape[0] and rows % (2 * SUBLANES) == 0
        cast_specs.append(pl.BlockSpec((rows, m.shape[1]), lambda i, j, nb=blocks: (jnp.minimum(j, nb - 1), 0)))
        out_shape.append(jax.ShapeDtypeStruct(m.shape, BF16))
    outs = pl.pallas_call(
        functools.partial(_ffn_body, final_norm=final_norm, emit_bf16=emit_bf16, n_cast=len(cast_jobs)),
        grid=(n // tm, steps),
        in_specs=[
            pl.BlockSpec((tm, d), lambda i, j: (i, 0)),
            pl.BlockSpec((1, d), lambda i, j: (0, 0)),
            up,
            up,
            down,
            pl.BlockSpec((1, d), lambda i, j: (0, 0)),
        ] + cast_specs,
        out_specs=out_specs + cast_specs,
        out_shape=out_shape,
        scratch_shapes=[pltpu.VMEM((tm, d), BF16)],
        compiler_params=_params("parallel", "arbitrary"),
        name="ffn",
    )(x, g, w1, w3, w2, gf, *cast_jobs)
    return outs if len(outs) > 1 else outs[0]


def _conv_taps(lc, w_ref):
    return [jnp.broadcast_to(w_ref[lc, j:j + 1, :], (SUBLANES, LANES)) for j in range(CONV_WIDTH)]


def _conv_windows(offsets):
    windows = {}
    for g, off in enumerate(offsets):
        for j in range(CONV_WIDTH):
            windows.setdefault(off + j, []).append((g, j))
    return windows


def _conv_groups(full_ref, lc, base, offsets, taps):
    head = CONV_PAD - CONV_HIST
    accs = [jnp.zeros((SUBLANES, LANES), F32) for _ in offsets]
    for woff, uses in _conv_windows(offsets).items():
        xo = full_ref[lc, pl.ds(base + (head + woff), SUBLANES), :]
        for g, j in uses:
            accs[g] = accs[g] + xo * taps[j]
    return jnp.concatenate(accs, axis=0)


def _ln_silu(cv, b_ref, lng_ref, lnb_ref):
    cv = cv + b_ref[...]
    mu = jnp.mean(cv, axis=-1, keepdims=True)
    var = jnp.mean(jnp.square(cv - mu), axis=-1, keepdims=True)
    return jax.nn.silu((cv - mu) * lax.rsqrt(var + EPS) * lng_ref[...] + lnb_ref[...])


def _inproj_body(*refs, n_heads, d_ret, d_conv, fuse_conv, steps_per_seq, conv_rt):
    if fuse_conv:
        (x_ref, g_ref, w_ref, cos_ref, sin_ref, hist_ref, cw_ref, cb_ref, lng_ref, lnb_ref,
         q_ref, k_ref, v_ref, gate_ref, y_ref, nhist_ref, h_ref, full_ref, cv_ref) = refs
    else:
        x_ref, g_ref, w_ref, cos_ref, sin_ref, q_ref, k_ref, v_ref, gate_ref, u_ref = refs
    tm = x_ref.shape[0]
    n_lc = d_conv // LANES
    head = CONV_PAD - CONV_HIST
    dq = n_heads * RET_DK
    col_a = 2 * dq + 2 * d_ret

    def proj(h, col, width):
        return jnp.dot(h, w_ref[:, col:col + width], preferred_element_type=F32)

    def rotary(t):
        cos, sin = cos_ref[...], sin_ref[...]
        t1, t2 = t[:, :ROPE_HALF], t[:, ROPE_HALF:]
        return jnp.concatenate([t1 * cos - t2 * sin, t2 * cos + t1 * sin], axis=-1)

    def project(h, kind, blk):
        lo = blk * RET_DK
        if kind == "q":
            q_ref[:, lo:lo + RET_DK] = (rotary(proj(h, lo, RET_DK)) * (RET_DK ** -0.5)).astype(BF16)
        elif kind == "k":
            k_ref[:, lo:lo + RET_DK] = rotary(proj(h, dq + lo, RET_DK)).astype(BF16)
        elif kind == "v":
            v_ref[:, lo:lo + RET_DK] = proj(h, 2 * dq + lo, RET_DK).astype(BF16)
        else:
            gate_ref[:, lo:lo + RET_DK] = proj(h, 2 * dq + d_ret + lo, RET_DK).astype(BF16)

    sections = [(kind, blk) for kind, width in (("q", dq), ("k", dq), ("v", d_ret), ("g", d_ret))
                for blk in range(width // RET_DK)]

    if not fuse_conv:
        h = _rms(x_ref[...], g_ref[...]).astype(BF16)
        for kind, blk in sections:
            project(h, kind, blk)
        u_ref[...] = proj(h, col_a, d_conv) * jax.nn.sigmoid(proj(h, col_a + d_conv, d_conv))
        return

    t = pl.program_id(0) % steps_per_seq

    @pl.when(t == 0)
    def _():
        for lc in range(n_lc):
            lanes = slice(lc * LANES, (lc + 1) * LANES)
            full_ref[lc, 0:SUBLANES, :] = jnp.zeros((SUBLANES, LANES), F32)
            full_ref[lc, head:CONV_PAD, :] = hist_ref[:, lanes]

    @pl.when(t > 0)
    def _():
        for lc in range(n_lc):
            full_ref[lc, head:CONV_PAD, :] = full_ref[lc, tm + head:tm + CONV_PAD, :]

    h_ref[...] = _rms(x_ref[...], g_ref[...]).astype(BF16)

    lc_per_blk = RET_DK // LANES
    for c in range(d_conv // RET_DK):
        h = h_ref[...]
        lo = c * RET_DK
        u = proj(h, col_a + lo, RET_DK) * jax.nn.sigmoid(proj(h, col_a + d_conv + lo, RET_DK))
        for s in range(lc_per_blk):
            full_ref[c * lc_per_blk + s, CONV_PAD:CONV_PAD + tm, :] = u[:, s * LANES:(s + 1) * LANES]

    offsets = [r * SUBLANES for r in range(conv_rt // SUBLANES)]
    units = [(r, lc) for r in range(tm // conv_rt) for lc in range(n_lc)]
    per_block = -(-len(units) // len(sections))

    def layer_norm(r):
        rows = slice(r * conv_rt, (r + 1) * conv_rt)
        cv = jnp.concatenate([cv_ref[lc, rows, :] for lc in range(n_lc)], axis=-1)
        y_ref[rows, :] = _ln_silu(cv, cb_ref, lng_ref, lnb_ref).astype(y_ref.dtype)

    done_rows = 0
    normed = 0
    for i, (kind, blk) in enumerate(sections):
        project(h_ref[...], kind, blk)
        for r, lc in units[i * per_block:(i + 1) * per_block]:
            cv_ref[lc, r * conv_rt:(r + 1) * conv_rt, :] = _conv_groups(
                full_ref, lc, r * conv_rt, offsets, _conv_taps(lc, cw_ref))
        for r in range(normed, done_rows):
            layer_norm(r)
        normed = done_rows
        done_rows = min((i + 1) * per_block, len(units)) // n_lc
    for r in range(normed, tm // conv_rt):
        layer_norm(r)
    for lc in range(n_lc):
        nhist_ref[:, lc * LANES:(lc + 1) * LANES] = full_ref[lc, tm + head:tm + CONV_PAD, :]


def _inproj(x, g, w_in, cos, sin, *, n_heads, d_ret, d_conv, tm, conv=None):
    n, d = x.shape
    period = cos.shape[0] // tm
    dq = n_heads * RET_DK
    row = lambda i: (i, 0)
    in_specs = [
        pl.BlockSpec((tm, d), row),
        _resident((1, d)),
        _resident(w_in.shape),
        pl.BlockSpec((tm, ROPE_HALF), lambda i: (i % period, 0)),
        pl.BlockSpec((tm, ROPE_HALF), lambda i: (i % period, 0)),
    ]
    out_specs = [pl.BlockSpec((tm, dq), row), pl.BlockSpec((tm, dq), row),
                 pl.BlockSpec((tm, d_ret), row), pl.BlockSpec((tm, d_ret), row),
                 pl.BlockSpec((tm, d_conv), row)]
    out_shape = [jax.ShapeDtypeStruct((n, dq), BF16), jax.ShapeDtypeStruct((n, dq), BF16),
                 jax.ShapeDtypeStruct((n, d_ret), BF16), jax.ShapeDtypeStruct((n, d_ret), BF16)]
    args = [x, g, w_in, cos, sin]
    scratch = []
    steps_per_seq = conv_rt = None
    if conv is None:
        out_shape.append(jax.ShapeDtypeStruct((n, d_conv), F32))
    else:
        hist, layer, seq, cw, cb, lng, lnb, conv_rt = conv
        assert seq % tm == 0 and tm % conv_rt == 0
        steps_per_seq = seq // tm
        cw = cw.reshape(cw.shape[0], d_conv // LANES, LANES).transpose(1, 0, 2)
        in_specs += [pl.BlockSpec((None, None, CONV_HIST, d_conv), lambda i: (layer, i // steps_per_seq, 0, 0)),
                     _resident(cw.shape), _resident(cb.shape), _resident(lng.shape), _resident(lnb.shape)]
        args += [hist, cw, cb, lng, lnb]
        out_specs.append(pl.BlockSpec((None, None, CONV_HIST, d_conv), lambda i: (0, i // steps_per_seq, 0, 0)))
        out_shape += [jax.ShapeDtypeStruct((n, d_conv), BF16),
                      jax.ShapeDtypeStruct((1, hist.shape[1], CONV_HIST, d_conv), F32)]
        scratch = [pltpu.VMEM((tm, d), BF16),
                   pltpu.VMEM((d_conv // LANES, tm + CONV_PAD, LANES), F32),
                   pltpu.VMEM((d_conv // LANES, tm, LANES), F32)]
    return pl.pallas_call(
        functools.partial(_inproj_body, n_heads=n_heads, d_ret=d_ret, d_conv=d_conv,
                          fuse_conv=conv is not None, steps_per_seq=steps_per_seq, conv_rt=conv_rt),
        grid=(n // tm,),
        in_specs=in_specs,
        out_specs=out_specs,
        out_shape=out_shape,
        scratch_shapes=scratch,
        compiler_params=_params("arbitrary" if conv is not None else "parallel"),
        name="inproj",
    )(*args)


def _retention_body(q_ref, k_ref, v_ref, gate_ref, s0_ref, dmask_ref, qdec_ref, kdec_ref,
                    cdec_ref, gng_ref, gnb_ref, *rest, n_heads, nb, nc, c, steps, fuse_out):
    if fuse_out:
        x_ref, cy_ref, w_ref, o_ref, so_ref, y_ref = rest
    else:
        y_ref, so_ref = rest
    if steps == 1:
        state_ref = s0_ref
    else:
        state_ref = so_ref

        @pl.when(pl.program_id(1) == 0)
        def _():
            so_ref[...] = s0_ref[...]

    narrow = c % (2 * SUBLANES) != 0
    mm = F32 if narrow else BF16
    nt = (((1,), (1,)), ((), ()))
    tn = (((0,), (0,)), ((), ()))

    for hd in range(n_heads):
        cols = slice(hd * RET_DK, (hd + 1) * RET_DK)
        vcols = slice(hd * RET_DV, (hd + 1) * RET_DV)
        dmask = dmask_ref[hd]
        qdec = qdec_ref[hd]
        kdec = kdec_ref[hd]
        cdec = cdec_ref[hd]
        gng = gng_ref[:, vcols]
        gnb = gnb_ref[:, vcols]
        if narrow:
            q_all = q_ref[:, cols].astype(F32)
            k_all = k_ref[:, cols].astype(F32)
            v_all = v_ref[:, vcols].astype(F32)
            gate_all = gate_ref[:, vcols].astype(F32)
        for bb in range(nb):
            state = state_ref[bb, hd]
            for cc in range(nc):
                rows = slice((bb * nc + cc) * c, (bb * nc + cc + 1) * c)
                if narrow:
                    q, k, v, gate = q_all[rows], k_all[rows], v_all[rows], gate_all[rows]
                else:
                    q, k, v = q_ref[rows, cols], k_ref[rows, cols], v_ref[rows, vcols]
                    gate = gate_ref[rows, vcols].astype(F32)
                s = lax.dot_general(q, k, nt, preferred_element_type=F32) * dmask
                inner = jnp.dot(s.astype(mm), v, preferred_element_type=F32)
                qd = (q.astype(F32) * qdec).astype(mm)
                cross = jnp.dot(qd, state.astype(mm), preferred_element_type=F32)
                o = inner + cross
                kd = (k.astype(F32) * kdec).astype(mm)
                state = state * cdec + lax.dot_general(kd, v, tn, preferred_element_type=F32)
                mu = jnp.mean(o, axis=-1, keepdims=True)
                var = jnp.mean(jnp.square(o - mu), axis=-1, keepdims=True)
                on = (o - mu) * lax.rsqrt(var + EPS) * gng + gnb
                y_ref[rows, vcols] = (jax.nn.silu(gate) * on).astype(y_ref.dtype)
            so_ref[bb, hd] = state

    if fuse_out:
        d_ret = n_heads * RET_DV
        y = jnp.dot(cy_ref[...], w_ref[d_ret:, :], preferred_element_type=F32)
        y += jnp.dot(y_ref[...], w_ref[:d_ret, :], preferred_element_type=F32)
        o_ref[...] = x_ref[...] + y


def _retention(q, k, v, gate, state0, gng, gnb, *, layer, seq, c, nb, nc, out=None):
    n, d_ret = v.shape
    batch, n_heads = state0.shape[1:3]
    steps = seq // (nc * c)
    assert nb == 1 or steps == 1

    lg = jnp.log1p(-jnp.exp2(-5.0 - jnp.arange(n_heads, dtype=F32)))
    idx = jnp.arange(c, dtype=F32)
    diff = idx[:, None] - idx[None, :]
    causal = diff >= 0
    dmask = jnp.where(causal[None], jnp.exp(jnp.where(causal, diff, 0.0)[None] * lg[:, None, None]), 0.0)
    qdec = jnp.broadcast_to(jnp.exp((idx[None, :] + 1.0) * lg[:, None])[:, :, None], (n_heads, c, RET_DK))
    kdec = jnp.broadcast_to(jnp.exp((c - 1.0 - idx[None, :]) * lg[:, None])[:, :, None], (n_heads, c, RET_DK))
    cdec = jnp.broadcast_to(jnp.exp(c * lg)[:, None, None], (n_heads, 1, RET_DV))

    rows = nb * nc * c
    row = lambda b, t: (b * steps + t, 0)
    st_blk = (None, nb) + state0.shape[2:]
    y_dtype = BF16 if c % (2 * SUBLANES) == 0 else F32
    in_specs = [
        pl.BlockSpec((rows, q.shape[1]), row),
        pl.BlockSpec((rows, k.shape[1]), row),
        pl.BlockSpec((rows, d_ret), row),
        pl.BlockSpec((rows, d_ret), row),
        pl.BlockSpec(st_blk, lambda b, t: (layer, b, 0, 0, 0)),
        _resident(dmask.shape),
        _resident(qdec.shape),
        _resident(kdec.shape),
        _resident(cdec.shape),
        _resident(gng.shape),
        _resident(gnb.shape),
    ]
    args = [q, k, v, gate, state0, dmask, qdec, kdec, cdec, gng, gnb]
    state_spec = pl.BlockSpec(st_blk, lambda b, t: (0, b, 0, 0, 0))
    state_shape = jax.ShapeDtypeStruct((1,) + state0.shape[1:], state0.dtype)
    scratch = []
    if out is None:
        out_specs = [pl.BlockSpec((rows, d_ret), row), state_spec]
        out_shape = [jax.ShapeDtypeStruct((n, d_ret), y_dtype), state_shape]
    else:
        x, conv_y, w_out = out
        assert y_dtype == BF16 and conv_y.dtype == BF16
        in_specs += [pl.BlockSpec((rows, x.shape[1]), row), pl.BlockSpec((rows, conv_y.shape[1]), row),
                     _resident(w_out.shape)]
        args += [x, conv_y, w_out]
        out_specs = [pl.BlockSpec((rows, x.shape[1]), row), state_spec]
        out_shape = [jax.ShapeDtypeStruct(x.shape, F32), state_shape]
        scratch = [pltpu.VMEM((rows, d_ret), BF16)]
    return pl.pallas_call(
        functools.partial(_retention_body, n_heads=n_heads, nb=nb, nc=nc, c=c, steps=steps,
                          fuse_out=out is not None),
        grid=(batch // nb, steps),
        in_specs=in_specs,
        out_specs=out_specs,
        out_shape=out_shape,
        scratch_shapes=scratch,
        compiler_params=_params("parallel", "arbitrary"),
        name="retention",
    )(*args)


def _conv_body(u_ref, buf_ref, w_ref, b_ref, lng_ref, lnb_ref, y_ref, nbuf_ref, full_ref, cv_ref,
               *, bb, seq):
    d = u_ref.shape[-1]
    n_lc = d // LANES
    head = CONV_PAD - CONV_HIST
    seg = seq + CONV_PAD

    def conv_tile(offsets):
        def conv_lanes(lc, carry):
            cv_ref[lc] = _conv_groups(full_ref, lc, 0, offsets, _conv_taps(lc, w_ref))
            return carry

        lax.fori_loop(0, n_lc, conv_lanes, 0)
        cv = jnp.concatenate([cv_ref[lc] for lc in range(n_lc)], axis=-1)
        return _ln_silu(cv, b_ref, lng_ref, lnb_ref).astype(y_ref.dtype)

    for b in range(bb):
        for lc in range(n_lc):
            lanes = slice(lc * LANES, (lc + 1) * LANES)
            full_ref[lc, b * seg:b * seg + SUBLANES, :] = jnp.zeros((SUBLANES, LANES), F32)
            full_ref[lc, b * seg + head:b * seg + CONV_PAD, :] = buf_ref[b, :, lanes]
            full_ref[lc, b * seg + CONV_PAD:(b + 1) * seg, :] = u_ref[b, :, lanes]

    y_ref[...] = conv_tile([b * seg for b in range(bb)]).reshape(bb, seq, d)

    for b in range(bb):
        for lc in range(n_lc):
            lanes = slice(lc * LANES, (lc + 1) * LANES)
            nbuf_ref[b, :, lanes] = full_ref[lc, b * seg + seq + head:(b + 1) * seg, :]


def _conv(u, buf, w, b, lng, lnb, *, layer, bb):
    batch, seq, d = u.shape
    assert seq == SUBLANES
    w = w.reshape(w.shape[0], d // LANES, LANES).transpose(1, 0, 2)
    y_dtype = BF16 if seq % (2 * SUBLANES) == 0 else F32
    blk = lambda i: (i, 0, 0)
    hist = lambda i: (layer, i, 0, 0)
    return pl.pallas_call(
        functools.partial(_conv_body, bb=bb, seq=seq),
        grid=(batch // bb,),
        in_specs=[
            pl.BlockSpec((bb, seq, d), blk),
            pl.BlockSpec((None, bb, CONV_HIST, d), hist),
            _resident(w.shape),
            _resident(b.shape),
            _resident(lng.shape),
            _resident(lnb.shape),
        ],
        out_specs=[
            pl.BlockSpec((bb, seq, d), blk),
            pl.BlockSpec((None, bb, CONV_HIST, d), lambda i: (0, i, 0, 0)),
        ],
        out_shape=[
            jax.ShapeDtypeStruct((batch, seq, d), y_dtype),
            jax.ShapeDtypeStruct((1, batch, CONV_HIST, d), u.dtype),
        ],
        scratch_shapes=[pltpu.VMEM((d // LANES, bb * (seq + CONV_PAD), LANES), F32),
                        pltpu.VMEM((d // LANES, bb * seq, LANES), F32)],
        compiler_params=_params("parallel"),
        name="conv",
    )(u, buf, w, b, lng, lnb)


def _outproj_body(x_ref, r_ref, c_ref, w_ref, o_ref, *, d_ret):
    y = jnp.dot(r_ref[...].astype(BF16), w_ref[:d_ret, :], preferred_element_type=F32)
    y += jnp.dot(c_ref[...].astype(BF16), w_ref[d_ret:, :], preferred_element_type=F32)
    o_ref[...] = x_ref[...] + y


def _outproj(x, ret_y, conv_y, w_out, *, tm):
    n, d = x.shape
    d_ret, d_conv = ret_y.shape[1], conv_y.shape[1]
    row = lambda i: (i, 0)
    return pl.pallas_call(
        functools.partial(_outproj_body, d_ret=d_ret),
        grid=(n // tm,),
        in_specs=[
            pl.BlockSpec((tm, d), row),
            pl.BlockSpec((tm, d_ret), row),
            pl.BlockSpec((tm, d_conv), row),
            _resident(w_out.shape),
        ],
        out_specs=pl.BlockSpec((tm, d), row),
        out_shape=jax.ShapeDtypeStruct((n, d), F32),
        compiler_params=_params("parallel"),
        name="outproj",
    )(x, ret_y, conv_y, w_out)


def _rope_tables(pos, rows):
    inv_freq = ROPE_BASE ** (-jnp.arange(ROPE_HALF, dtype=F32) / ROPE_HALF)
    ang = pos[:, None] * inv_freq[None, :]
    reps = max(1, rows // pos.shape[0])
    return jnp.tile(jnp.cos(ang), (reps, 1)), jnp.tile(jnp.sin(ang), (reps, 1))


def _stream(x, pos, ret_state, conv_buf, wts, ffn1_w, ffn2_w, *, layer, emit_bf16, ffn_tm, tf, tm,
            ret_nb, ret_nc, conv_bb, conv_rt):
    batch, seq, d = x.shape
    n_heads = ret_state.shape[2]
    d_ret = n_heads * RET_DV
    d_conv = conv_buf.shape[-1]
    c = min(seq, RET_CHUNK)
    xf = x.reshape(batch * seq, d)

    x1 = _ffn(xf, wts["norm_ffn1_g"], *ffn1_w, wts["norm_final_g"], final_norm=False,
              tm=ffn_tm, tf=tf, emit_bf16=emit_bf16,
              cast_jobs=(wts["w_in"], wts["w_out"]) if emit_bf16 else ())
    if emit_bf16:
        x1, *ffn1_w = x1
        *ffn1_w, w_in, w_out = ffn1_w
        wts = dict(wts, w_in=w_in, w_out=w_out)
    cos, sin = _rope_tables(pos, tm)
    conv_w = (wts["conv_w"], wts["conv_b"], wts["conv_ln_g"], wts["conv_ln_b"])
    fuse_conv = seq % tm == 0
    q, k, v, gate, *conv_out = _inproj(
        x1, wts["norm_mix_g"], wts["w_in"], cos, sin, n_heads=n_heads, d_ret=d_ret, d_conv=d_conv, tm=tm,
        conv=(conv_buf, layer, seq, *conv_w, conv_rt) if fuse_conv else None)
    retention = functools.partial(_retention, q, k, v, gate, ret_state, wts["ret_gn_g"], wts["ret_gn_b"],
                                  layer=layer, seq=seq, c=c, nb=ret_nb, nc=ret_nc)
    if fuse_conv:
        conv_y, new_buf = conv_out
        x2, new_ret = retention(out=(x1, conv_y, wts["w_out"]))
    else:
        ret_y, new_ret = retention()
        conv_y, new_buf = _conv(conv_out[0].reshape(batch, seq, d_conv), conv_buf, *conv_w,
                                layer=layer, bb=conv_bb)
        x2 = _outproj(x1, ret_y, conv_y.reshape(batch * seq, d_conv), wts["w_out"], tm=tm)
    y = _ffn(x2, wts["norm_ffn2_g"], *ffn2_w, wts["norm_final_g"], final_norm=True,
             tm=ffn_tm, tf=tf, emit_bf16=emit_bf16)
    if emit_bf16:
        y, *ffn2_w = y
    return y.reshape(batch, seq, d), new_ret, new_buf, wts, tuple(ffn1_w), tuple(ffn2_w)


def kernel(x_prompt, x_sample, state_retention, state_conv, norm_ffn1_g, ffn1_w1, ffn1_w3, ffn1_w2, norm_mix_g, w_in, ret_gn_g, ret_gn_b, conv_w, conv_b, conv_ln_g, conv_ln_b, w_out, norm_ffn2_g, ffn2_w1, ffn2_w3, ffn2_w2, norm_final_g):
    depth = w_in.shape[0]
    assert depth == 1
    layer = 0
    row = lambda g: g.reshape(1, -1)
    wts = {
        "norm_ffn1_g": row(norm_ffn1_g[layer]),
        "norm_mix_g": row(norm_mix_g[layer]),
        "w_in": w_in[layer],
        "ret_gn_g": row(ret_gn_g[layer]),
        "ret_gn_b": row(ret_gn_b[layer]),
        "conv_w": conv_w[layer],
        "conv_b": row(conv_b[layer]),
        "conv_ln_g": row(conv_ln_g[layer]),
        "conv_ln_b": row(conv_ln_b[layer]),
        "w_out": w_out[layer],
        "norm_ffn2_g": row(norm_ffn2_g[layer]),
        "norm_final_g": row(norm_final_g),
    }
    bp, tp, _ = x_prompt.shape
    bs, ts, _ = x_sample.shape
    n_heads = state_retention.shape[2]
    d_conv = state_conv.shape[-1]

    pos_p = jnp.arange(tp, dtype=F32)
    pos_s = PAST_LEN + jnp.arange(ts, dtype=F32)
    r0 = jnp.zeros((1, bp, n_heads, RET_DK, RET_DV), state_retention.dtype)
    c0 = jnp.zeros((1, bp, CONV_HIST, d_conv), x_prompt.dtype)

    ffn1_w = (ffn1_w1[layer], ffn1_w3[layer], ffn1_w2[layer])
    ffn2_w = (ffn2_w1[layer], ffn2_w3[layer], ffn2_w2[layer])
    y_s, ret_s, conv_s, wts, ffn1_w, ffn2_w = _stream(
        x_sample, pos_s, state_retention, state_conv, wts, ffn1_w, ffn2_w, layer=layer,
        emit_bf16=True, ffn_tm=1024, tf=256, tm=512, ret_nb=8, ret_nc=1, conv_bb=16, conv_rt=None)
    y_p, ret_p, conv_p, _, _, _ = _stream(
        x_prompt, pos_p, r0, c0, wts, ffn1_w, ffn2_w, layer=0,
        emit_bf16=False, ffn_tm=1024, tf=512, tm=512, ret_nb=1, ret_nc=4, conv_bb=None, conv_rt=128)
    return (y_p, y_s, ret_p, conv_p, ret_s, conv_s)
```

```python
import functools

import jax
import jax.numpy as jnp
from jax import lax
from jax.experimental import pallas as pl
from jax.experimental.pallas import tpu as pltpu

F32 = jnp.float32
BF16 = jnp.bfloat16

RET_DK = 256
RET_DV = 256
RET_CHUNK = 128
CONV_WIDTH = 31
ROPE_BASE = 10000.0
EPS = 1e-6
PAST_LEN = 16384

LANES = 128
SUBLANES = 8
VMEM_LIMIT_BYTES = 60 * 1024 * 1024

ROPE_HALF = RET_DK // 2
CONV_HIST = CONV_WIDTH - 1
CONV_PAD = 32
ROW_CHUNK = 128
CAST_ROWS = 128


def _rms(x, g):
    return x * lax.rsqrt(jnp.mean(x * x, axis=-1, keepdims=True) + EPS) * g


def _params(*semantics):
    return pltpu.CompilerParams(dimension_semantics=semantics,
                                vmem_limit_bytes=VMEM_LIMIT_BYTES)


def _resident(shape):
    nd = len(shape)
    return pl.BlockSpec(shape, lambda *_: (0,) * nd, pipeline_mode=pl.Buffered(1))


def _ffn_body(x_ref, g_ref, w1_ref, w3_ref, w2_ref, gf_ref, *rest, final_norm, emit_bf16, n_cast):
    n_emit = 3 if emit_bf16 else 0
    cast_src = rest[:n_cast]
    o_ref = rest[n_cast]
    emit_refs = rest[n_cast + 1:n_cast + 1 + n_emit]
    cast_dst = rest[n_cast + 1 + n_emit:n_cast + 1 + n_emit + n_cast]
    h_ref = rest[-1]
    j = pl.program_id(1)

    def step(first, last):
        if first:
            h = _rms(x_ref[...], g_ref[...]).astype(BF16)
            h_ref[...] = h
        else:
            h = h_ref[...]
        w1, w3, w2 = (w_ref[...].astype(BF16) for w_ref in (w1_ref, w3_ref, w2_ref))
        for dst_ref, w in zip(emit_refs, (w1, w3, w2)):
            dst_ref[...] = w
        for src_ref, dst_ref in zip(cast_src, cast_dst):
            dst_ref[...] = src_ref[...].astype(BF16)
        a = jnp.dot(h, w1, preferred_element_type=F32)
        b = jnp.dot(h, w3, preferred_element_type=F32)
        act = (jax.nn.silu(a) * b).astype(BF16)
        part = jnp.dot(act, w2, preferred_element_type=F32)
        if first:
            o_ref[...] = part
        elif last and not final_norm:
            o_ref[...] = x_ref[...] + 0.5 * (o_ref[...] + part)
        else:
            o_ref[...] += part

    n_steps = pl.num_programs(1)

    @pl.when(j == 0)
    def _():
        step(True, False)

    @pl.when(jnp.logical_and(j > 0, j < n_steps - 1))
    def _():
        step(False, False)

    @pl.when(j == n_steps - 1)
    def _():
        step(False, True)
        if final_norm:
            def finish(r, carry):
                rows = pl.ds(pl.multiple_of(r * ROW_CHUNK, ROW_CHUNK), ROW_CHUNK)
                o_ref[rows, :] = _rms(x_ref[rows, :] + 0.5 * o_ref[rows, :], gf_ref[...])
                return carry
            lax.fori_loop(0, x_ref.shape[0] // ROW_CHUNK, finish, 0)


def _ffn(x, g, w1, w3, w2, gf, *, final_norm, tm, tf, emit_bf16=False, cast_jobs=()):
    n, d = x.shape
    f = w1.shape[1]
    steps = f // tf
    assert steps >= 2
    up = pl.BlockSpec((d, tf), lambda i, j: (0, j))
    down = pl.BlockSpec((tf, d), lambda i, j: (j, 0))
    out_specs = [pl.BlockSpec((tm, d), lambda i, j: (i, 0))]
    out_shape = [jax.ShapeDtypeStruct((n, d), F32)]
    if emit_bf16:
        out_specs += [up, up, down]
        out_shape += [jax.ShapeDtypeStruct(w.shape, BF16) for w in (w1, w3, w2)]
    cast_specs = []
    for m in cast_jobs:
        blocks = min(steps, m.shape[0] // CAST_ROWS)
        rows = m.shape[0] // blocks
        assert rows * blocks == m.shape[0] and rows % (2 * SUBLANES) == 0
        cast_specs.append(pl.BlockSpec((rows, m.shape[1]), lambda i, j, nb=blocks: (jnp.minimum(j, nb - 1), 0)))
        out_shape.append(jax.ShapeDtypeStruct(m.shape, BF16))
    outs = pl.pallas_call(
        functools.partial(_ffn_body, final_norm=final_norm, emit_bf16=emit_bf16, n_cast=len(cast_jobs)),
        grid=(n // tm, steps),
        in_specs=[
            pl.BlockSpec((tm, d), lambda i, j: (i, 0)),
            pl.BlockSpec((1, d), lambda i, j: (0, 0)),
            up,
            up,
            down,
            pl.BlockSpec((1, d), lambda i, j: (0, 0)),
        ] + cast_specs,
        out_specs=out_specs + cast_specs,
        out_shape=out_shape,
        scratch_shapes=[pltpu.VMEM((tm, d), BF16)],
        compiler_params=_params("parallel", "arbitrary"),
        name="ffn",
    )(x, g, w1, w3, w2, gf, *cast_jobs)
    return outs if len(outs) > 1 else outs[0]


def _conv_taps(lc, w_ref):
    return [jnp.broadcast_to(w_ref[lc, j:j + 1, :], (SUBLANES, LANES)) for j in range(CONV_WIDTH)]


def _conv_windows(offsets):
    windows = {}
    for g, off in enumerate(offsets):
        for j in range(CONV_WIDTH):
            windows.setdefault(off + j, []).append((g, j))
    return windows


def _conv_groups(full_ref, lc, base, offsets, taps):
    head = CONV_PAD - CONV_HIST
    accs = [jnp.zeros((SUBLANES, LANES), F32) for _ in offsets]
    for woff, uses in _conv_windows(offsets).items():
        xo = full_ref[lc, pl.ds(base + (head + woff), SUBLANES), :]
        for g, j in uses:
            accs[g] = accs[g] + xo * taps[j]
    return jnp.concatenate(accs, axis=0)


def _ln_silu(cv, b_ref, lng_ref, lnb_ref):
    cv = cv + b_ref[...]
    mu = jnp.mean(cv, axis=-1, keepdims=True)
    var = jnp.mean(jnp.square(cv - mu), axis=-1, keepdims=True)
    return jax.nn.silu((cv - mu) * lax.rsqrt(var + EPS) * lng_ref[...] + lnb_ref[...])


def _inproj_body(*refs, n_heads, d_ret, d_conv, fuse_conv, steps_per_seq, conv_rt):
    if fuse_conv:
        (x_ref, g_ref, w_ref, cos_ref, sin_ref, hist_ref, cw_ref, cb_ref, lng_ref, lnb_ref,
         q_ref, k_ref, v_ref, gate_ref, y_ref, nhist_ref, h_ref, full_ref, cv_ref) = refs
    else:
        x_ref, g_ref, w_ref, cos_ref, sin_ref, q_ref, k_ref, v_ref, gate_ref, u_ref = refs
    tm = x_ref.shape[0]
    n_lc = d_conv // LANES
    head = CONV_PAD - CONV_HIST
    dq = n_heads * RET_DK
    col_a = 2 * dq + 2 * d_ret

    def proj(h, col, width):
        return jnp.dot(h, w_ref[:, col:col + width], preferred_element_type=F32)

    def rotary(t):
        cos, sin = cos_ref[...], sin_ref[...]
        t1, t2 = t[:, :ROPE_HALF], t[:, ROPE_HALF:]
        return jnp.concatenate([t1 * cos - t2 * sin, t2 * cos + t1 * sin], axis=-1)

    def project(h, kind, blk):
        lo = blk * RET_DK
        if kind == "q":
            q_ref[:, lo:lo + RET_DK] = (rotary(proj(h, lo, RET_DK)) * (RET_DK ** -0.5)).astype(BF16)
        elif kind == "k":
            k_ref[:, lo:lo + RET_DK] = rotary(proj(h, dq + lo, RET_DK)).astype(BF16)
        elif kind == "v":
            v_ref[:, lo:lo + RET_DK] = proj(h, 2 * dq + lo, RET_DK).astype(BF16)
        else:
            gate_ref[:, lo:lo + RET_DK] = proj(h, 2 * dq + d_ret + lo, RET_DK).astype(BF16)

    sections = [(kind, blk) for kind, width in (("q", dq), ("k", dq), ("v", d_ret), ("g", d_ret))
                for blk in range(width // RET_DK)]

    if not fuse_conv:
        h = _rms(x_ref[...], g_ref[...]).astype(BF16)
        for kind, blk in sections:
            project(h, kind, blk)
        u_ref[...] = proj(h, col_a, d_conv) * jax.nn.sigmoid(proj(h, col_a + d_conv, d_conv))
        return

    t = pl.program_id(0) % steps_per_seq

    @pl.when(t == 0)
    def _():
        for lc in range(n_lc):
            lanes = slice(lc * LANES, (lc + 1) * LANES)
            full_ref[lc, 0:SUBLANES, :] = jnp.zeros((SUBLANES, LANES), F32)
            full_ref[lc, head:CONV_PAD, :] = hist_ref[:, lanes]

    @pl.when(t > 0)
    def _():
        for lc in range(n_lc):
            full_ref[lc, head:CONV_PAD, :] = full_ref[lc, tm + head:tm + CONV_PAD, :]

    h_ref[...] = _rms(x_ref[...], g_ref[...]).astype(BF16)

    lc_per_blk = RET_DK // LANES
    for c in range(d_conv // RET_DK):
        h = h_ref[...]
        lo = c * RET_DK
        u = proj(h, col_a + lo, RET_DK) * jax.nn.sigmoid(proj(h, col_a + d_conv + lo, RET_DK))
        for s in range(lc_per_blk):
            full_ref[c * lc_per_blk + s, CONV_PAD:CONV_PAD + tm, :] = u[:, s * LANES:(s + 1) * LANES]

    offsets = [r * SUBLANES for r in range(conv_rt // SUBLANES)]
    units = [(r, lc) for r in range(tm // conv_rt) for lc in range(n_lc)]
    per_block = -(-len(units) // len(sections))

    def layer_norm(r):
        rows = slice(r * conv_rt, (r + 1) * conv_rt)
        cv = jnp.concatenate([cv_ref[lc, rows, :] for lc in range(n_lc)], axis=-1)
        y_ref[rows, :] = _ln_silu(cv, cb_ref, lng_ref, lnb_ref).astype(y_ref.dtype)

    done_rows = 0
    normed = 0
    for i, (kind, blk) in enumerate(sections):
        project(h_ref[...], kind, blk)
        for r, lc in units[i * per_block:(i + 1) * per_block]:
            cv_ref[lc, r * conv_rt:(r + 1) * conv_rt, :] = _conv_groups(
                full_ref, lc, r * conv_rt, offsets, _conv_taps(lc, cw_ref))
        for r in range(normed, done_rows):
            layer_norm(r)
        normed = done_rows
        done_rows = min((i + 1) * per_block, len(units)) // n_lc
    for r in range(normed, tm // conv_rt):
        layer_norm(r)
    for lc in range(n_lc):
        nhist_ref[:, lc * LANES:(lc + 1) * LANES] = full_ref[lc, tm + head:tm + CONV_PAD, :]


def _inproj(x, g, w_in, cos, sin, *, n_heads, d_ret, d_conv, tm, conv=None):
    n, d = x.shape
    period = cos.shape[0] // tm
    dq = n_heads * RET_DK
    row = lambda i: (i, 0)
    in_specs = [
        pl.BlockSpec((tm, d), row),
        _resident((1, d)),
        _resident(w_in.shape),
        pl.BlockSpec((tm, ROPE_HALF), lambda i: (i % period, 0)),
        pl.BlockSpec((tm, ROPE_HALF), lambda i: (i % period, 0)),
    ]
    out_specs = [pl.BlockSpec((tm, dq), row), pl.BlockSpec((tm, dq), row),
                 pl.BlockSpec((tm, d_ret), row), pl.BlockSpec((tm, d_ret), row),
                 pl.BlockSpec((tm, d_conv), row)]
    out_shape = [jax.ShapeDtypeStruct((n, dq), BF16), jax.ShapeDtypeStruct((n, dq), BF16),
                 jax.ShapeDtypeStruct((n, d_ret), BF16), jax.ShapeDtypeStruct((n, d_ret), BF16)]
    args = [x, g, w_in, cos, sin]
    scratch = []
    steps_per_seq = conv_rt = None
    if conv is None:
        out_shape.append(jax.ShapeDtypeStruct((n, d_conv), F32))
    else:
        hist, layer, seq, cw, cb, lng, lnb, conv_rt = conv
        assert seq % tm == 0 and tm % conv_rt == 0
        steps_per_seq = seq // tm
        cw = cw.reshape(cw.shape[0], d_conv // LANES, LANES).transpose(1, 0, 2)
        in_specs += [pl.BlockSpec((None, None, CONV_HIST, d_conv), lambda i: (layer, i // steps_per_seq, 0, 0)),
                     _resident(cw.shape), _resident(cb.shape), _resident(lng.shape), _resident(lnb.shape)]
        args += [hist, cw, cb, lng, lnb]
        out_specs.append(pl.BlockSpec((None, None, CONV_HIST, d_conv), lambda i: (0, i // steps_per_seq, 0, 0)))
        out_shape += [jax.ShapeDtypeStruct((n, d_conv), BF16),
                      jax.ShapeDtypeStruct((1, hist.shape[1], CONV_HIST, d_conv), F32)]
        scratch = [pltpu.VMEM((tm, d), BF16),
                   pltpu.VMEM((d_conv // LANES, tm + CONV_PAD, LANES), F32),
                   pltpu.VMEM((d_conv // LANES, tm, LANES), F32)]
    return pl.pallas_call(
        functools.partial(_inproj_body, n_heads=n_heads, d_ret=d_ret, d_conv=d_conv,
                          fuse_conv=conv is not None, steps_per_seq=steps_per_seq, conv_rt=conv_rt),
        grid=(n // tm,),
        in_specs=in_specs,
        out_specs=out_specs,
        out_shape=out_shape,
        scratch_shapes=scratch,
        compiler_params=_params("arbitrary" if conv is not None else "parallel"),
        name="inproj",
    )(*args)


def _retention_body(q_ref, k_ref, v_ref, gate_ref, s0_ref, dmask_ref, qdec_ref, kdec_ref,
                    cdec_ref, gng_ref, gnb_ref, x_ref, cy_ref, w_ref, o_ref, so_ref, y_ref,
                    *, n_heads, nb, nc, c, steps):
    if steps == 1:
        state_ref = s0_ref
    else:
        state_ref = so_ref

        @pl.when(pl.program_id(1) == 0)
        def _():
            so_ref[...] = s0_ref[...]

    narrow = c % (2 * SUBLANES) != 0
    mm = F32 if narrow else BF16
    nt = (((1,), (1,)), ((), ()))
    tn = (((0,), (0,)), ((), ()))

    for hd in range(n_heads):
        cols = slice(hd * RET_DK, (hd + 1) * RET_DK)
        vcols = slice(hd * RET_DV, (hd + 1) * RET_DV)
        dmask = dmask_ref[hd]
        qdec = qdec_ref[hd]
        kdec = kdec_ref[hd]
        cdec = cdec_ref[hd]
        gng = gng_ref[:, vcols]
        gnb = gnb_ref[:, vcols]
        if narrow:
            q_all = q_ref[:, cols].astype(F32)
            k_all = k_ref[:, cols].astype(F32)
            v_all = v_ref[:, vcols].astype(F32)
            gate_all = gate_ref[:, vcols].astype(F32)
        for bb in range(nb):
            state = state_ref[bb, hd]
            for cc in range(nc):
                rows = slice((bb * nc + cc) * c, (bb * nc + cc + 1) * c)
                if narrow:
                    q, k, v, gate = q_all[rows], k_all[rows], v_all[rows], gate_all[rows]
                else:
                    q, k, v = q_ref[rows, cols], k_ref[rows, cols], v_ref[rows, vcols]
                    gate = gate_ref[rows, vcols].astype(F32)
                s = lax.dot_general(q, k, nt, preferred_element_type=F32) * dmask
                inner = jnp.dot(s.astype(mm), v, preferred_element_type=F32)
                qd = (q.astype(F32) * qdec).astype(mm)
                cross = jnp.dot(qd, state.astype(mm), preferred_element_type=F32)
                o = inner + cross
                kd = (k.astype(F32) * kdec).astype(mm)
                state = state * cdec + lax.dot_general(kd, v, tn, preferred_element_type=F32)
                mu = jnp.mean(o, axis=-1, keepdims=True)
                var = jnp.mean(jnp.square(o - mu), axis=-1, keepdims=True)
                on = (o - mu) * lax.rsqrt(var + EPS) * gng + gnb
                y_ref[rows, vcols] = (jax.nn.silu(gate) * on).astype(y_ref.dtype)
            so_ref[bb, hd] = state

    d_ret = n_heads * RET_DV
    y = jnp.dot(cy_ref[...].astype(BF16), w_ref[d_ret:, :], preferred_element_type=F32)
    y += jnp.dot(y_ref[...].astype(BF16), w_ref[:d_ret, :], preferred_element_type=F32)
    o_ref[...] = x_ref[...] + y


def _retention(q, k, v, gate, state0, gng, gnb, out, *, layer, seq, c, nb, nc):
    n, d_ret = v.shape
    batch, n_heads = state0.shape[1:3]
    steps = seq // (nc * c)
    assert nb == 1 or steps == 1

    lg = jnp.log1p(-jnp.exp2(-5.0 - jnp.arange(n_heads, dtype=F32)))
    idx = jnp.arange(c, dtype=F32)
    diff = idx[:, None] - idx[None, :]
    causal = diff >= 0
    dmask = jnp.where(causal[None], jnp.exp(jnp.where(causal, diff, 0.0)[None] * lg[:, None, None]), 0.0)
    qdec = jnp.broadcast_to(jnp.exp((idx[None, :] + 1.0) * lg[:, None])[:, :, None], (n_heads, c, RET_DK))
    kdec = jnp.broadcast_to(jnp.exp((c - 1.0 - idx[None, :]) * lg[:, None])[:, :, None], (n_heads, c, RET_DK))
    cdec = jnp.broadcast_to(jnp.exp(c * lg)[:, None, None], (n_heads, 1, RET_DV))

    rows = nb * nc * c
    row = lambda b, t: (b * steps + t, 0)
    st_blk = (None, nb) + state0.shape[2:]
    y_dtype = BF16 if c % (2 * SUBLANES) == 0 else F32
    in_specs = [
        pl.BlockSpec((rows, q.shape[1]), row),
        pl.BlockSpec((rows, k.shape[1]), row),
        pl.BlockSpec((rows, d_ret), row),
        pl.BlockSpec((rows, d_ret), row),
        pl.BlockSpec(st_blk, lambda b, t: (layer, b, 0, 0, 0)),
        _resident(dmask.shape),
        _resident(qdec.shape),
        _resident(kdec.shape),
        _resident(cdec.shape),
        _resident(gng.shape),
        _resident(gnb.shape),
    ]
    args = [q, k, v, gate, state0, dmask, qdec, kdec, cdec, gng, gnb]
    state_spec = pl.BlockSpec(st_blk, lambda b, t: (0, b, 0, 0, 0))
    state_shape = jax.ShapeDtypeStruct((1,) + state0.shape[1:], state0.dtype)
    x, conv_y, w_out = out
    in_specs += [pl.BlockSpec((rows, x.shape[1]), row), pl.BlockSpec((rows, conv_y.shape[1]), row),
                 _resident(w_out.shape)]
    args += [x, conv_y, w_out]
    return pl.pallas_call(
        functools.partial(_retention_body, n_heads=n_heads, nb=nb, nc=nc, c=c, steps=steps),
        grid=(batch // nb, steps),
        in_specs=in_specs,
        out_specs=[pl.BlockSpec((rows, x.shape[1]), row), state_spec],
        out_shape=[jax.ShapeDtypeStruct(x.shape, F32), state_shape],
        scratch_shapes=[pltpu.VMEM((rows, d_ret), y_dtype)],
        compiler_params=_params("parallel", "arbitrary"),
        name="retention_outproj",
    )(*args)


def _conv_body(u_ref, buf_ref, w_ref, b_ref, lng_ref, lnb_ref, y_ref, nbuf_ref, full_ref, cv_ref,
               *, bb, seq):
    d = u_ref.shape[-1]
    n_lc = d // LANES
    head = CONV_PAD - CONV_HIST
    seg = seq + CONV_PAD

    def conv_tile(offsets):
        def conv_lanes(lc, carry):
            cv_ref[lc] = _conv_groups(full_ref, lc, 0, offsets, _conv_taps(lc, w_ref))
            return carry

        lax.fori_loop(0, n_lc, conv_lanes, 0)
        cv = jnp.concatenate([cv_ref[lc] for lc in range(n_lc)], axis=-1)
        return _ln_silu(cv, b_ref, lng_ref, lnb_ref).astype(y_ref.dtype)

    for b in range(bb):
        for lc in range(n_lc):
            lanes = slice(lc * LANES, (lc + 1) * LANES)
            full_ref[lc, b * seg:b * seg + SUBLANES, :] = jnp.zeros((SUBLANES, LANES), F32)
            full_ref[lc, b * seg + head:b * seg + CONV_PAD, :] = buf_ref[b, :, lanes]
            full_ref[lc, b * seg + CONV_PAD:(b + 1) * seg, :] = u_ref[b, :, lanes]

    y_ref[...] = conv_tile([b * seg for b in range(bb)]).reshape(bb, seq, d)

    for b in range(bb):
        for lc in range(n_lc):
            lanes = slice(lc * LANES, (lc + 1) * LANES)
            nbuf_ref[b, :, lanes] = full_ref[lc, b * seg + seq + head:(b + 1) * seg, :]


def _conv(u, buf, w, b, lng, lnb, *, layer, bb):
    batch, seq, d = u.shape
    assert seq == SUBLANES
    w = w.reshape(w.shape[0], d // LANES, LANES).transpose(1, 0, 2)
    y_dtype = BF16 if seq % (2 * SUBLANES) == 0 else F32
    blk = lambda i: (i, 0, 0)
    hist = lambda i: (layer, i, 0, 0)
    return pl.pallas_call(
        functools.partial(_conv_body, bb=bb, seq=seq),
        grid=(batch // bb,),
        in_specs=[
            pl.BlockSpec((bb, seq, d), blk),
            pl.BlockSpec((None, bb, CONV_HIST, d), hist),
            _resident(w.shape),
            _resident(b.shape),
            _resident(lng.shape),
            _resident(lnb.shape),
        ],
        out_specs=[
            pl.BlockSpec((bb, seq, d), blk),
            pl.BlockSpec((None, bb, CONV_HIST, d), lambda i: (0, i, 0, 0)),
        ],
        out_shape=[
            jax.ShapeDtypeStruct((batch, seq, d), y_dtype),
            jax.ShapeDtypeStruct((1, batch, CONV_HIST, d), u.dtype),
        ],
        scratch_shapes=[pltpu.VMEM((d // LANES, bb * (seq + CONV_PAD), LANES), F32),
                        pltpu.VMEM((d // LANES, bb * seq, LANES), F32)],
        compiler_params=_params("parallel"),
        name="conv",
    )(u, buf, w, b, lng, lnb)


def _rope_tables(pos, rows):
    inv_freq = ROPE_BASE ** (-jnp.arange(ROPE_HALF, dtype=F32) / ROPE_HALF)
    ang = pos[:, None] * inv_freq[None, :]
    reps = max(1, rows // pos.shape[0])
    return jnp.tile(jnp.cos(ang), (reps, 1)), jnp.tile(jnp.sin(ang), (reps, 1))


def _stream(x, pos, ret_state, conv_buf, wts, ffn1_w, ffn2_w, *, layer, emit_bf16, ffn_tm, tf, tm,
            ret_nb, ret_nc, conv_bb, conv_rt):
    batch, seq, d = x.shape
    n_heads = ret_state.shape[2]
    d_ret = n_heads * RET_DV
    d_conv = conv_buf.shape[-1]
    c = min(seq, RET_CHUNK)
    xf = x.reshape(batch * seq, d)

    x1 = _ffn(xf, wts["norm_ffn1_g"], *ffn1_w, wts["norm_final_g"], final_norm=False,
              tm=ffn_tm, tf=tf, emit_bf16=emit_bf16,
              cast_jobs=(wts["w_in"], wts["w_out"]) if emit_bf16 else ())
    if emit_bf16:
        x1, *ffn1_w = x1
        *ffn1_w, w_in, w_out = ffn1_w
        wts = dict(wts, w_in=w_in, w_out=w_out)
    cos, sin = _rope_tables(pos, tm)
    conv_w = (wts["conv_w"], wts["conv_b"], wts["conv_ln_g"], wts["conv_ln_b"])
    fuse_conv = seq % tm == 0
    q, k, v, gate, *conv_out = _inproj(
        x1, wts["norm_mix_g"], wts["w_in"], cos, sin, n_heads=n_heads, d_ret=d_ret, d_conv=d_conv, tm=tm,
        conv=(conv_buf, layer, seq, *conv_w, conv_rt) if fuse_conv else None)
    if fuse_conv:
        conv_y, new_buf = conv_out
    else:
        conv_y, new_buf = _conv(conv_out[0].reshape(batch, seq, d_conv), conv_buf, *conv_w,
                                layer=layer, bb=conv_bb)
        conv_y = conv_y.reshape(batch * seq, d_conv)
    x2, new_ret = _retention(q, k, v, gate, ret_state, wts["ret_gn_g"], wts["ret_gn_b"],
                             (x1, conv_y, wts["w_out"]), layer=layer, seq=seq, c=c, nb=ret_nb, nc=ret_nc)
    y = _ffn(x2, wts["norm_ffn2_g"], *ffn2_w, wts["norm_final_g"], final_norm=True,
             tm=ffn_tm, tf=tf, emit_bf16=emit_bf16)
    if emit_bf16:
        y, *ffn2_w = y
    return y.reshape(batch, seq, d), new_ret, new_buf, wts, tuple(ffn1_w), tuple(ffn2_w)


def kernel(x_prompt, x_sample, state_retention, state_conv, norm_ffn1_g, ffn1_w1, ffn1_w3, ffn1_w2, norm_mix_g, w_in, ret_gn_g, ret_gn_b, conv_w, conv_b, conv_ln_g, conv_ln_b, w_out, norm_ffn2_g, ffn2_w1, ffn2_w3, ffn2_w2, norm_final_g):
    depth = w_in.shape[0]
    assert depth == 1
    layer = 0
    row = lambda g: g.reshape(1, -1)
    wts = {
        "norm_ffn1_g": row(norm_ffn1_g[layer]),
        "norm_mix_g": row(norm_mix_g[layer]),
        "w_in": w_in[layer],
        "ret_gn_g": row(ret_gn_g[layer]),
        "ret_gn_b": row(ret_gn_b[layer]),
        "conv_w": conv_w[layer],
        "conv_b": row(conv_b[layer]),
        "conv_ln_g": row(conv_ln_g[layer]),
        "conv_ln_b": row(conv_ln_b[layer]),
        "w_out": w_out[layer],
        "norm_ffn2_g": row(norm_ffn2_g[layer]),
        "norm_final_g": row(norm_final_g),
    }
    bp, tp, _ = x_prompt.shape
    bs, ts, _ = x_sample.shape
    n_heads = state_retention.shape[2]
    d_conv = state_conv.shape[-1]

    pos_p = jnp.arange(tp, dtype=F32)
    pos_s = PAST_LEN + jnp.arange(ts, dtype=F32)
    r0 = jnp.zeros((1, bp, n_heads, RET_DK, RET_DV), state_retention.dtype)
    c0 = jnp.zeros((1, bp, CONV_HIST, d_conv), x_prompt.dtype)

    ffn1_w = (ffn1_w1[layer], ffn1_w3[layer], ffn1_w2[layer])
    ffn2_w = (ffn2_w1[layer], ffn2_w3[layer], ffn2_w2[layer])
    y_s, ret_s, conv_s, wts, ffn1_w, ffn2_w = _stream(
        x_sample, pos_s, state_retention, state_conv, wts, ffn1_w, ffn2_w, layer=layer,
        emit_bf16=True, ffn_tm=1024, tf=256, tm=512, ret_nb=8, ret_nc=1, conv_bb=16, conv_rt=None)
    y_p, ret_p, conv_p, _, _, _ = _stream(
        x_prompt, pos_p, r0, c0, wts, ffn1_w, ffn2_w, layer=0,
        emit_bf16=False, ffn_tm=1024, tf=512, tm=512, ret_nb=1, ret_nc=4, conv_bb=None, conv_rt=128)
    return (y_p, y_s, ret_p, conv_p, ret_s, conv_s)
```

```python
import functools

import jax
import jax.numpy as jnp
from jax import lax
from jax.experimental import pallas as pl
from jax.experimental.pallas import tpu as pltpu

F32 = jnp.float32
BF16 = jnp.bfloat16

RET_DK = 256
RET_DV = 256
RET_CHUNK = 128
CONV_WIDTH = 31
ROPE_BASE = 10000.0
EPS = 1e-6
PAST_LEN = 16384

LANES = 128
SUBLANES = 8
VMEM_LIMIT_BYTES = 60 * 1024 * 1024

ROPE_HALF = RET_DK // 2
CONV_HIST = CONV_WIDTH - 1
CONV_PAD = 32
ROW_CHUNK = 128
CAST_ROWS = 128


def _rms(x, g):
    return x * lax.rsqrt(jnp.mean(x * x, axis=-1, keepdims=True) + EPS) * g


def _params(*semantics):
    return pltpu.CompilerParams(dimension_semantics=semantics,
                                vmem_limit_bytes=VMEM_LIMIT_BYTES)


def _resident(shape):
    nd = len(shape)
    return pl.BlockSpec(shape, lambda *_: (0,) * nd, pipeline_mode=pl.Buffered(1))


def _ffn_body(x_ref, g_ref, w1_ref, w3_ref, w2_ref, gf_ref, *rest, final_norm, emit_bf16, n_cast):
    n_emit = 3 if emit_bf16 else 0
    cast_src = rest[:n_cast]
    o_ref = rest[n_cast]
    emit_refs = rest[n_cast + 1:n_cast + 1 + n_emit]
    cast_dst = rest[n_cast + 1 + n_emit:n_cast + 1 + n_emit + n_cast]
    h_ref = rest[-1]
    j = pl.program_id(1)

    def step(first, last):
        if first:
            h = _rms(x_ref[...], g_ref[...]).astype(BF16)
            h_ref[...] = h
        else:
            h = h_ref[...]
        w1, w3, w2 = (w_ref[...].astype(BF16) for w_ref in (w1_ref, w3_ref, w2_ref))
        for dst_ref, w in zip(emit_refs, (w1, w3, w2)):
            dst_ref[...] = w
        for src_ref, dst_ref in zip(cast_src, cast_dst):
            dst_ref[...] = src_ref[...].astype(BF16)
        a = jnp.dot(h, w1, preferred_element_type=F32)
        b = jnp.dot(h, w3, preferred_element_type=F32)
        act = (jax.nn.silu(a) * b).astype(BF16)
        part = jnp.dot(act, w2, preferred_element_type=F32)
        if first:
            o_ref[...] = part
        elif last and not final_norm:
            o_ref[...] = x_ref[...] + 0.5 * (o_ref[...] + part)
        else:
            o_ref[...] += part

    n_steps = pl.num_programs(1)

    @pl.when(j == 0)
    def _():
        step(True, False)

    @pl.when(jnp.logical_and(j > 0, j < n_steps - 1))
    def _():
        step(False, False)

    @pl.when(j == n_steps - 1)
    def _():
        step(False, True)
        if final_norm:
            def finish(r, carry):
                rows = pl.ds(pl.multiple_of(r * ROW_CHUNK, ROW_CHUNK), ROW_CHUNK)
                o_ref[rows, :] = _rms(x_ref[rows, :] + 0.5 * o_ref[rows, :], gf_ref[...])
                return carry
            lax.fori_loop(0, x_ref.shape[0] // ROW_CHUNK, finish, 0)


def _ffn(x, g, w1, w3, w2, gf, *, final_norm, tm, tf, emit_bf16=False, cast_jobs=()):
    n, d = x.shape
    f = w1.shape[1]
    steps = f // tf
    assert steps >= 2
    up = pl.BlockSpec((d, tf), lambda i, j: (0, j))
    down = pl.BlockSpec((tf, d), lambda i, j: (j, 0))
    out_specs = [pl.BlockSpec((tm, d), lambda i, j: (i, 0))]
    out_shape = [jax.ShapeDtypeStruct((n, d), F32)]
    if emit_bf16:
        out_specs += [up, up, down]
        out_shape += [jax.ShapeDtypeStruct(w.shape, BF16) for w in (w1, w3, w2)]
    cast_specs = []
    for m in cast_jobs:
        blocks = min(steps, m.shape[0] // CAST_ROWS)
        rows = m.shape[0] // blocks
        assert rows * blocks == m.shape[0] and rows % (2 * SUBLANES) == 0
        cast_specs.append(pl.BlockSpec((rows, m.shape[1]), lambda i, j, nb=blocks: (jnp.minimum(j, nb - 1), 0)))
        out_shape.append(jax.ShapeDtypeStruct(m.shape, BF16))
    outs = pl.pallas_call(
        functools.partial(_ffn_body, final_norm=final_norm, emit_bf16=emit_bf16, n_cast=len(cast_jobs)),
        grid=(n // tm, steps),
        in_specs=[
            pl.BlockSpec((tm, d), lambda i, j: (i, 0)),
            pl.BlockSpec((1, d), lambda i, j: (0, 0)),
            up,
            up,
            down,
            pl.BlockSpec((1, d), lambda i, j: (0, 0)),
        ] + cast_specs,
        out_specs=out_specs + cast_specs,
        out_shape=out_shape,
        scratch_shapes=[pltpu.VMEM((tm, d), BF16)],
        compiler_params=_params("parallel", "arbitrary"),
        name="ffn",
    )(x, g, w1, w3, w2, gf, *cast_jobs)
    return outs if len(outs) > 1 else outs[0]


def _conv_taps(lc, w_ref):
    return [jnp.broadcast_to(w_ref[lc, j:j + 1, :], (SUBLANES, LANES)) for j in range(CONV_WIDTH)]


def _conv_windows(offsets):
    windows = {}
    for g, off in enumerate(offsets):
        for j in range(CONV_WIDTH):
            windows.setdefault(off + j, []).append((g, j))
    return windows


def _conv_groups(full_ref, lc, base, offsets, taps):
    head = CONV_PAD - CONV_HIST
    accs = [jnp.zeros((SUBLANES, LANES), F32) for _ in offsets]
    for woff, uses in _conv_windows(offsets).items():
        xo = full_ref[lc, pl.ds(base + (head + woff), SUBLANES), :]
        for g, j in uses:
            accs[g] = accs[g] + xo * taps[j]
    return jnp.concatenate(accs, axis=0)


def _ln_silu(cv, b_ref, lng_ref, lnb_ref):
    cv = cv + b_ref[...]
    mu = jnp.mean(cv, axis=-1, keepdims=True)
    var = jnp.mean(jnp.square(cv - mu), axis=-1, keepdims=True)
    return jax.nn.silu((cv - mu) * lax.rsqrt(var + EPS) * lng_ref[...] + lnb_ref[...])


def _inproj_body(*refs, n_heads, d_ret, d_conv, fuse_conv, steps_per_seq, conv_rt):
    if fuse_conv:
        (x_ref, g_ref, w_ref, cos_ref, sin_ref, hist_ref, cw_ref, cb_ref, lng_ref, lnb_ref,
         q_ref, k_ref, v_ref, gate_ref, y_ref, nhist_ref, h_ref, full_ref, cv_ref) = refs
    else:
        x_ref, g_ref, w_ref, cos_ref, sin_ref, q_ref, k_ref, v_ref, gate_ref, u_ref = refs
    tm = x_ref.shape[0]
    n_lc = d_conv // LANES
    head = CONV_PAD - CONV_HIST
    dq = n_heads * RET_DK
    col_a = 2 * dq + 2 * d_ret

    def proj(h, col, width):
        return jnp.dot(h, w_ref[:, col:col + width], preferred_element_type=F32)

    def rotary(t):
        cos, sin = cos_ref[...], sin_ref[...]
        t1, t2 = t[:, :ROPE_HALF], t[:, ROPE_HALF:]
        return jnp.concatenate([t1 * cos - t2 * sin, t2 * cos + t1 * sin], axis=-1)

    def project(h, kind, blk):
        lo = blk * RET_DK
        if kind == "q":
            q_ref[:, lo:lo + RET_DK] = (rotary(proj(h, lo, RET_DK)) * (RET_DK ** -0.5)).astype(BF16)
        elif kind == "k":
            k_ref[:, lo:lo + RET_DK] = rotary(proj(h, dq + lo, RET_DK)).astype(BF16)
        elif kind == "v":
            v_ref[:, lo:lo + RET_DK] = proj(h, 2 * dq + lo, RET_DK).astype(BF16)
        else:
            gate_ref[:, lo:lo + RET_DK] = proj(h, 2 * dq + d_ret + lo, RET_DK).astype(BF16)

    sections = [(kind, blk) for kind, width in (("q", dq), ("k", dq), ("v", d_ret), ("g", d_ret))
                for blk in range(width // RET_DK)]

    if not fuse_conv:
        h = _rms(x_ref[...], g_ref[...]).astype(BF16)
        for kind, blk in sections:
            project(h, kind, blk)
        u_ref[...] = proj(h, col_a, d_conv) * jax.nn.sigmoid(proj(h, col_a + d_conv, d_conv))
        return

    t = pl.program_id(0) % steps_per_seq

    @pl.when(t == 0)
    def _():
        for lc in range(n_lc):
            lanes = slice(lc * LANES, (lc + 1) * LANES)
            full_ref[lc, 0:SUBLANES, :] = jnp.zeros((SUBLANES, LANES), F32)
            full_ref[lc, head:CONV_PAD, :] = hist_ref[:, lanes]

    @pl.when(t > 0)
    def _():
        for lc in range(n_lc):
            full_ref[lc, head:CONV_PAD, :] = full_ref[lc, tm + head:tm + CONV_PAD, :]

    h_ref[...] = _rms(x_ref[...], g_ref[...]).astype(BF16)

    lc_per_blk = RET_DK // LANES
    for c in range(d_conv // RET_DK):
        h = h_ref[...]
        lo = c * RET_DK
        u = proj(h, col_a + lo, RET_DK) * jax.nn.sigmoid(proj(h, col_a + d_conv + lo, RET_DK))
        for s in range(lc_per_blk):
            full_ref[c * lc_per_blk + s, CONV_PAD:CONV_PAD + tm, :] = u[:, s * LANES:(s + 1) * LANES]

    offsets = [r * SUBLANES for r in range(conv_rt // SUBLANES)]
    units = [(r, lc) for r in range(tm // conv_rt) for lc in range(n_lc)]
    per_block = -(-len(units) // len(sections))

    def layer_norm(r):
        rows = slice(r * conv_rt, (r + 1) * conv_rt)
        cv = jnp.concatenate([cv_ref[lc, rows, :] for lc in range(n_lc)], axis=-1)
        y_ref[rows, :] = _ln_silu(cv, cb_ref, lng_ref, lnb_ref).astype(y_ref.dtype)

    done_rows = 0
    normed = 0
    for i, (kind, blk) in enumerate(sections):
        project(h_ref[...], kind, blk)
        for r, lc in units[i * per_block:(i + 1) * per_block]:
            cv_ref[lc, r * conv_rt:(r + 1) * conv_rt, :] = _conv_groups(
                full_ref, lc, r * conv_rt, offsets, _conv_taps(lc, cw_ref))
        for r in range(normed, done_rows):
            layer_norm(r)
        normed = done_rows
        done_rows = min((i + 1) * per_block, len(units)) // n_lc
    for r in range(normed, tm // conv_rt):
        layer_norm(r)
    for lc in range(n_lc):
        nhist_ref[:, lc * LANES:(lc + 1) * LANES] = full_ref[lc, tm + head:tm + CONV_PAD, :]


def _inproj(x, g, w_in, cos, sin, *, n_heads, d_ret, d_conv, tm, conv=None):
    n, d = x.shape
    period = cos.shape[0] // tm
    dq = n_heads * RET_DK
    row = lambda i: (i, 0)
    in_specs = [
        pl.BlockSpec((tm, d), row),
        _resident((1, d)),
        _resident(w_in.shape),
        pl.BlockSpec((tm, ROPE_HALF), lambda i: (i % period, 0)),
        pl.BlockSpec((tm, ROPE_HALF), lambda i: (i % period, 0)),
    ]
    out_specs = [pl.BlockSpec((tm, dq), row), pl.BlockSpec((tm, dq), row),
                 pl.BlockSpec((tm, d_ret), row), pl.BlockSpec((tm, d_ret), row),
                 pl.BlockSpec((tm, d_conv), row)]
    out_shape = [jax.ShapeDtypeStruct((n, dq), BF16), jax.ShapeDtypeStruct((n, dq), BF16),
                 jax.ShapeDtypeStruct((n, d_ret), BF16), jax.ShapeDtypeStruct((n, d_ret), BF16)]
    args = [x, g, w_in, cos, sin]
    scratch = []
    steps_per_seq = conv_rt = None
    if conv is None:
        out_shape.append(jax.ShapeDtypeStruct((n, d_conv), F32))
    else:
        hist, layer, seq, cw, cb, lng, lnb, conv_rt = conv
        assert seq % tm == 0 and tm % conv_rt == 0
        steps_per_seq = seq // tm
        cw = cw.reshape(cw.shape[0], d_conv // LANES, LANES).transpose(1, 0, 2)
        in_specs += [pl.BlockSpec((None, None, CONV_HIST, d_conv), lambda i: (layer, i // steps_per_seq, 0, 0)),
                     _resident(cw.shape), _resident(cb.shape), _resident(lng.shape), _resident(lnb.shape)]
        args += [hist, cw, cb, lng, lnb]
        out_specs.append(pl.BlockSpec((None, None, CONV_HIST, d_conv), lambda i: (0, i // steps_per_seq, 0, 0)))
        out_shape += [jax.ShapeDtypeStruct((n, d_conv), BF16),
                      jax.ShapeDtypeStruct((1, hist.shape[1], CONV_HIST, d_conv), F32)]
        scratch = [pltpu.VMEM((tm, d), BF16),
                   pltpu.VMEM((d_conv // LANES, tm + CONV_PAD, LANES), F32),
                   pltpu.VMEM((d_conv // LANES, tm, LANES), F32)]
    return pl.pallas_call(
        functools.partial(_inproj_body, n_heads=n_heads, d_ret=d_ret, d_conv=d_conv,
                          fuse_conv=conv is not None, steps_per_seq=steps_per_seq, conv_rt=conv_rt),
        grid=(n // tm,),
        in_specs=in_specs,
        out_specs=out_specs,
        out_shape=out_shape,
        scratch_shapes=scratch,
        compiler_params=_params("arbitrary" if conv is not None else "parallel"),
        name="inproj",
    )(*args)


def _retention_body(q_ref, k_ref, v_ref, gate_ref, s0_ref, dmask_ref, qdec_ref, kdec_ref,
                    cdec_ref, gng_ref, gnb_ref, x_ref, cy_ref, w_ref, o_ref, so_ref, y_ref,
                    *, n_heads, nb, nc, c, steps):
    if steps == 1:
        state_ref = s0_ref
    else:
        state_ref = so_ref

        @pl.when(pl.program_id(1) == 0)
        def _():
            so_ref[...] = s0_ref[...]

    narrow = c % (2 * SUBLANES) != 0
    mm = F32 if narrow else BF16
    nt = (((1,), (1,)), ((), ()))
    tn = (((0,), (0,)), ((), ()))

    for hd in range(n_heads):
        cols = slice(hd * RET_DK, (hd + 1) * RET_DK)
        vcols = slice(hd * RET_DV, (hd + 1) * RET_DV)
        dmask = dmask_ref[hd]
        qdec = qdec_ref[hd]
        kdec = kdec_ref[hd]
        cdec = cdec_ref[hd]
        gng = gng_ref[:, vcols]
        gnb = gnb_ref[:, vcols]
        if narrow:
            q_all = q_ref[:, cols].astype(F32)
            k_all = k_ref[:, cols].astype(F32)
            v_all = v_ref[:, vcols].astype(F32)
            gate_all = gate_ref[:, vcols].astype(F32)
        for bb in range(nb):
            state = state_ref[bb, hd]
            for cc in range(nc):
                rows = slice((bb * nc + cc) * c, (bb * nc + cc + 1) * c)
                if narrow:
                    q, k, v, gate = q_all[rows], k_all[rows], v_all[rows], gate_all[rows]
                else:
                    q, k, v = q_ref[rows, cols], k_ref[rows, cols], v_ref[rows, vcols]
                    gate = gate_ref[rows, vcols].astype(F32)
                s = lax.dot_general(q, k, nt, preferred_element_type=F32) * dmask
                inner = jnp.dot(s.astype(mm), v, preferred_element_type=F32)
                qd = (q.astype(F32) * qdec).astype(mm)
                cross = jnp.dot(qd, state.astype(mm), preferred_element_type=F32)
                o = inner + cross
                kd = (k.astype(F32) * kdec).astype(mm)
                state = state * cdec + lax.dot_general(kd, v, tn, preferred_element_type=F32)
                mu = jnp.mean(o, axis=-1, keepdims=True)
                var = jnp.mean(jnp.square(o - mu), axis=-1, keepdims=True)
                on = (o - mu) * lax.rsqrt(var + EPS) * gng + gnb
                y_ref[rows, vcols] = (jax.nn.silu(gate) * on).astype(y_ref.dtype)
            so_ref[bb, hd] = state

    d_ret = n_heads * RET_DV
    y = jnp.dot(cy_ref[...].astype(BF16), w_ref[d_ret:, :], preferred_element_type=F32)
    y += jnp.dot(y_ref[...].astype(BF16), w_ref[:d_ret, :], preferred_element_type=F32)
    o_ref[...] = x_ref[...] + y


def _retention(q, k, v, gate, state0, gng, gnb, out, *, layer, seq, c, nb, nc):
    n, d_ret = v.shape
    batch, n_heads = state0.shape[1:3]
    steps = seq // (nc * c)
    assert nb == 1 or steps == 1

    lg = jnp.log1p(-jnp.exp2(-5.0 - jnp.arange(n_heads, dtype=F32)))
    idx = jnp.arange(c, dtype=F32)
    diff = idx[:, None] - idx[None, :]
    causal = diff >= 0
    dmask = jnp.where(causal[None], jnp.exp(jnp.where(causal, diff, 0.0)[None] * lg[:, None, None]), 0.0)
    qdec = jnp.broadcast_to(jnp.exp((idx[None, :] + 1.0) * lg[:, None])[:, :, None], (n_heads, c, RET_DK))
    kdec = jnp.broadcast_to(jnp.exp((c - 1.0 - idx[None, :]) * lg[:, None])[:, :, None], (n_heads, c, RET_DK))
    cdec = jnp.broadcast_to(jnp.exp(c * lg)[:, None, None], (n_heads, 1, RET_DV))

    rows = nb * nc * c
    row = lambda b, t: (b * steps + t, 0)
    st_blk = (None, nb) + state0.shape[2:]
    y_dtype = BF16 if c % (2 * SUBLANES) == 0 else F32
    in_specs = [
        pl.BlockSpec((rows, q.shape[1]), row),
        pl.BlockSpec((rows, k.shape[1]), row),
        pl.BlockSpec((rows, d_ret), row),
        pl.BlockSpec((rows, d_ret), row),
        pl.BlockSpec(st_blk, lambda b, t: (layer, b, 0, 0, 0)),
        _resident(dmask.shape),
        _resident(qdec.shape),
        _resident(kdec.shape),
        _resident(cdec.shape),
        _resident(gng.shape),
        _resident(gnb.shape),
    ]
    args = [q, k, v, gate, state0, dmask, qdec, kdec, cdec, gng, gnb]
    state_spec = pl.BlockSpec(st_blk, lambda b, t: (0, b, 0, 0, 0))
    state_shape = jax.ShapeDtypeStruct((1,) + state0.shape[1:], state0.dtype)
    x, conv_y, w_out = out
    in_specs += [pl.BlockSpec((rows, x.shape[1]), row), pl.BlockSpec((rows, conv_y.shape[1]), row),
                 _resident(w_out.shape)]
    args += [x, conv_y, w_out]
    return pl.pallas_call(
        functools.partial(_retention_body, n_heads=n_heads, nb=nb, nc=nc, c=c, steps=steps),
        grid=(batch // nb, steps),
        in_specs=in_specs,
        out_specs=[pl.BlockSpec((rows, x.shape[1]), row), state_spec],
        out_shape=[jax.ShapeDtypeStruct(x.shape, F32), state_shape],
        scratch_shapes=[pltpu.VMEM((rows, d_ret), y_dtype)],
        compiler_params=_params("parallel", "arbitrary"),
        name="retention_outproj",
    )(*args)


def _conv_body(u_ref, hist_ref, w_ref, b_ref, lng_ref, lnb_ref, y_ref, nhist_ref, full_ref, cv_ref,
               stage_ref, *, bb, seq):
    d = u_ref.shape[-1]
    n_lc = d // LANES

    for lc in range(n_lc):
        lanes = slice(lc * LANES, (lc + 1) * LANES)
        for tau in range(CONV_HIST):
            full_ref[lc, tau] = hist_ref[tau, :, lanes]
        stage_ref[lc] = u_ref[:, lanes]
        for t in range(seq):
            full_ref[lc, CONV_HIST + t] = stage_ref[lc, pl.ds(t, bb, stride=seq), :]

    def conv_lanes(lc, carry):
        taps = [jnp.broadcast_to(w_ref[lc, j:j + 1, :], (bb, LANES)) for j in range(CONV_WIDTH)]
        accs = [jnp.zeros((bb, LANES), F32) for _ in range(seq)]
        for tau in range(CONV_HIST + seq):
            xo = full_ref[lc, tau]
            for t in range(seq):
                if 0 <= tau - t < CONV_WIDTH:
                    accs[t] = accs[t] + xo * taps[tau - t]
        for t in range(seq):
            cv_ref[lc, t] = accs[t]
        return carry

    lax.fori_loop(0, n_lc, conv_lanes, 0)
    cv = jnp.concatenate([cv_ref[lc].reshape(seq * bb, LANES) for lc in range(n_lc)], axis=-1)
    y = _ln_silu(cv, b_ref, lng_ref, lnb_ref)
    for lc in range(n_lc):
        lanes = slice(lc * LANES, (lc + 1) * LANES)
        for t in range(seq):
            stage_ref[lc, pl.ds(t, bb, stride=seq), :] = y[t * bb:(t + 1) * bb, lanes]
        y_ref[:, lanes] = stage_ref[lc]
        for tau in range(CONV_HIST):
            nhist_ref[tau, :, lanes] = full_ref[lc, seq + tau]


def _conv(u, hist, w, b, lng, lnb, *, layer, seq, bb):
    n, d = u.shape
    batch = n // seq
    assert seq <= CONV_HIST and bb % SUBLANES == 0
    w = w.reshape(w.shape[0], d // LANES, LANES).transpose(1, 0, 2)
    row = lambda i: (i, 0)
    return pl.pallas_call(
        functools.partial(_conv_body, bb=bb, seq=seq),
        grid=(batch // bb,),
        in_specs=[
            pl.BlockSpec((bb * seq, d), row),
            pl.BlockSpec((None, CONV_HIST, bb, d), lambda i: (layer, 0, i, 0)),
            _resident(w.shape),
            _resident(b.shape),
            _resident(lng.shape),
            _resident(lnb.shape),
        ],
        out_specs=[
            pl.BlockSpec((bb * seq, d), row),
            pl.BlockSpec((None, CONV_HIST, bb, d), lambda i: (0, 0, i, 0)),
        ],
        out_shape=[
            jax.ShapeDtypeStruct((n, d), F32),
            jax.ShapeDtypeStruct((1, CONV_HIST, batch, d), u.dtype),
        ],
        scratch_shapes=[pltpu.VMEM((d // LANES, CONV_HIST + seq, bb, LANES), F32),
                        pltpu.VMEM((d // LANES, seq, bb, LANES), F32),
                        pltpu.VMEM((d // LANES, bb * seq, LANES), F32)],
        compiler_params=_params("parallel"),
        name="conv",
    )(u, hist, w, b, lng, lnb)


def _rope_tables(pos, rows):
    inv_freq = ROPE_BASE ** (-jnp.arange(ROPE_HALF, dtype=F32) / ROPE_HALF)
    ang = pos[:, None] * inv_freq[None, :]
    reps = max(1, rows // pos.shape[0])
    return jnp.tile(jnp.cos(ang), (reps, 1)), jnp.tile(jnp.sin(ang), (reps, 1))


def _stream(x, pos, ret_state, conv_buf, wts, ffn1_w, ffn2_w, *, layer, emit_bf16, ffn_tm, tf, tm,
            ret_nb, ret_nc, conv_bb, conv_rt):
    batch, seq, d = x.shape
    n_heads = ret_state.shape[2]
    d_ret = n_heads * RET_DV
    d_conv = conv_buf.shape[-1]
    c = min(seq, RET_CHUNK)
    xf = x.reshape(batch * seq, d)

    x1 = _ffn(xf, wts["norm_ffn1_g"], *ffn1_w, wts["norm_final_g"], final_norm=False,
              tm=ffn_tm, tf=tf, emit_bf16=emit_bf16,
              cast_jobs=(wts["w_in"], wts["w_out"]) if emit_bf16 else ())
    if emit_bf16:
        x1, *ffn1_w = x1
        *ffn1_w, w_in, w_out = ffn1_w
        wts = dict(wts, w_in=w_in, w_out=w_out)
    cos, sin = _rope_tables(pos, tm)
    conv_w = (wts["conv_w"], wts["conv_b"], wts["conv_ln_g"], wts["conv_ln_b"])
    fuse_conv = seq % tm == 0
    q, k, v, gate, *conv_out = _inproj(
        x1, wts["norm_mix_g"], wts["w_in"], cos, sin, n_heads=n_heads, d_ret=d_ret, d_conv=d_conv, tm=tm,
        conv=(conv_buf, layer, seq, *conv_w, conv_rt) if fuse_conv else None)
    if fuse_conv:
        conv_y, new_buf = conv_out
    else:
        conv_y, new_buf = _conv(conv_out[0], conv_buf.transpose(0, 2, 1, 3), *conv_w,
                                layer=layer, seq=seq, bb=conv_bb)
        new_buf = new_buf.transpose(0, 2, 1, 3)
    x2, new_ret = _retention(q, k, v, gate, ret_state, wts["ret_gn_g"], wts["ret_gn_b"],
                             (x1, conv_y, wts["w_out"]), layer=layer, seq=seq, c=c, nb=ret_nb, nc=ret_nc)
    y = _ffn(x2, wts["norm_ffn2_g"], *ffn2_w, wts["norm_final_g"], final_norm=True,
             tm=ffn_tm, tf=tf, emit_bf16=emit_bf16)
    if emit_bf16:
        y, *ffn2_w = y
    return y.reshape(batch, seq, d), new_ret, new_buf, wts, tuple(ffn1_w), tuple(ffn2_w)


def kernel(x_prompt, x_sample, state_retention, state_conv, norm_ffn1_g, ffn1_w1, ffn1_w3, ffn1_w2, norm_mix_g, w_in, ret_gn_g, ret_gn_b, conv_w, conv_b, conv_ln_g, conv_ln_b, w_out, norm_ffn2_g, ffn2_w1, ffn2_w3, ffn2_w2, norm_final_g):
    depth = w_in.shape[0]
    assert depth == 1
    layer = 0
    row = lambda g: g.reshape(1, -1)
    wts = {
        "norm_ffn1_g": row(norm_ffn1_g[layer]),
        "norm_mix_g": row(norm_mix_g[layer]),
        "w_in": w_in[layer],
        "ret_gn_g": row(ret_gn_g[layer]),
        "ret_gn_b": row(ret_gn_b[layer]),
        "conv_w": conv_w[layer],
        "conv_b": row(conv_b[layer]),
        "conv_ln_g": row(conv_ln_g[layer]),
        "conv_ln_b": row(conv_ln_b[layer]),
        "w_out": w_out[layer],
        "norm_ffn2_g": row(norm_ffn2_g[layer]),
        "norm_final_g": row(norm_final_g),
    }
    bp, tp, _ = x_prompt.shape
    bs, ts, _ = x_sample.shape
    n_heads = state_retention.shape[2]
    d_conv = state_conv.shape[-1]

    pos_p = jnp.arange(tp, dtype=F32)
    pos_s = PAST_LEN + jnp.arange(ts, dtype=F32)
    r0 = jnp.zeros((1, bp, n_heads, RET_DK, RET_DV), state_retention.dtype)
    c0 = jnp.zeros((1, bp, CONV_HIST, d_conv), x_prompt.dtype)

    ffn1_w = (ffn1_w1[layer], ffn1_w3[layer], ffn1_w2[layer])
    ffn2_w = (ffn2_w1[layer], ffn2_w3[layer], ffn2_w2[layer])
    y_s, ret_s, conv_s, wts, ffn1_w, ffn2_w = _stream(
        x_sample, pos_s, state_retention, state_conv, wts, ffn1_w, ffn2_w, layer=layer,
        emit_bf16=True, ffn_tm=1024, tf=256, tm=512, ret_nb=8, ret_nc=1, conv_bb=16, conv_rt=None)
    y_p, ret_p, conv_p, _, _, _ = _stream(
        x_prompt, pos_p, r0, c0, wts, ffn1_w, ffn2_w, layer=0,
        emit_bf16=False, ffn_tm=1024, tf=512, tm=512, ret_nb=1, ret_nc=4, conv_bb=None, conv_rt=128)
    return (y_p, y_s, ret_p, conv_p, ret_s, conv_s)
```

```python
import functools

import jax
import jax.numpy as jnp
from jax import lax
from jax.experimental import pallas as pl
from jax.experimental.pallas import tpu as pltpu

F32 = jnp.float32
BF16 = jnp.bfloat16

RET_DK = 256
RET_DV = 256
RET_CHUNK = 128
CONV_WIDTH = 31
ROPE_BASE = 10000.0
EPS = 1e-6
PAST_LEN = 16384

LANES = 128
SUBLANES = 8
VMEM_LIMIT_BYTES = 60 * 1024 * 1024

ROPE_HALF = RET_DK // 2
CONV_HIST = CONV_WIDTH - 1
CONV_PAD = 32
ROW_CHUNK = 128
CAST_ROWS = 128
RING_SLOTS = 3


def _rms(x, g):
    return x * lax.rsqrt(jnp.mean(x * x, axis=-1, keepdims=True) + EPS) * g


def _params(*semantics):
    return pltpu.CompilerParams(dimension_semantics=semantics,
                                vmem_limit_bytes=VMEM_LIMIT_BYTES)


def _resident(shape):
    nd = len(shape)
    return pl.BlockSpec(shape, lambda *_: (0,) * nd, pipeline_mode=pl.Buffered(1))


def _ffn_body(x_ref, g_ref, w1_ref, w3_ref, w2_ref, gf_ref, *rest, final_norm, emit_bf16, n_cast):
    n_emit = 3 if emit_bf16 else 0
    cast_src = rest[:n_cast]
    o_ref = rest[n_cast]
    emit_refs = rest[n_cast + 1:n_cast + 1 + n_emit]
    cast_dst = rest[n_cast + 1 + n_emit:n_cast + 1 + n_emit + n_cast]
    h_ref = rest[-1]
    j = pl.program_id(1)

    def step(first, last):
        if first:
            h = _rms(x_ref[...], g_ref[...]).astype(BF16)
            h_ref[...] = h
        else:
            h = h_ref[...]
        w1, w3, w2 = (w_ref[...].astype(BF16) for w_ref in (w1_ref, w3_ref, w2_ref))
        for dst_ref, w in zip(emit_refs, (w1, w3, w2)):
            dst_ref[...] = w
        for src_ref, dst_ref in zip(cast_src, cast_dst):
            dst_ref[...] = src_ref[...].astype(BF16)
        a = jnp.dot(h, w1, preferred_element_type=F32)
        b = jnp.dot(h, w3, preferred_element_type=F32)
        act = (jax.nn.silu(a) * b).astype(BF16)
        part = jnp.dot(act, w2, preferred_element_type=F32)
        if first:
            o_ref[...] = part
        elif last and not final_norm:
            o_ref[...] = x_ref[...] + 0.5 * (o_ref[...] + part)
        else:
            o_ref[...] += part

    n_steps = pl.num_programs(1)

    @pl.when(j == 0)
    def _():
        step(True, False)

    @pl.when(jnp.logical_and(j > 0, j < n_steps - 1))
    def _():
        step(False, False)

    @pl.when(j == n_steps - 1)
    def _():
        step(False, True)
        if final_norm:
            def finish(r, carry):
                rows = pl.ds(pl.multiple_of(r * ROW_CHUNK, ROW_CHUNK), ROW_CHUNK)
                o_ref[rows, :] = _rms(x_ref[rows, :] + 0.5 * o_ref[rows, :], gf_ref[...])
                return carry
            lax.fori_loop(0, x_ref.shape[0] // ROW_CHUNK, finish, 0)


def _ffn(x, g, w1, w3, w2, gf, *, final_norm, tm, tf, emit_bf16=False, cast_jobs=()):
    n, d = x.shape
    f = w1.shape[1]
    steps = f // tf
    assert steps >= 2
    up = pl.BlockSpec((d, tf), lambda i, j: (0, j))
    down = pl.BlockSpec((tf, d), lambda i, j: (j, 0))
    out_specs = [pl.BlockSpec((tm, d), lambda i, j: (i, 0))]
    out_shape = [jax.ShapeDtypeStruct((n, d), F32)]
    if emit_bf16:
        out_specs += [up, up, down]
        out_shape += [jax.ShapeDtypeStruct(w.shape, BF16) for w in (w1, w3, w2)]
    cast_specs = []
    for m in cast_jobs:
        blocks = min(steps, m.shape[0] // CAST_ROWS)
        rows = m.shape[0] // blocks
        assert rows * blocks == m.shape[0] and rows % (2 * SUBLANES) == 0
        cast_specs.append(pl.BlockSpec((rows, m.shape[1]), lambda i, j, nb=blocks: (jnp.minimum(j, nb - 1), 0)))
        out_shape.append(jax.ShapeDtypeStruct(m.shape, BF16))
    outs = pl.pallas_call(
        functools.partial(_ffn_body, final_norm=final_norm, emit_bf16=emit_bf16, n_cast=len(cast_jobs)),
        grid=(n // tm, steps),
        in_specs=[
            pl.BlockSpec((tm, d), lambda i, j: (i, 0)),
            pl.BlockSpec((1, d), lambda i, j: (0, 0)),
            up,
            up,
            down,
            pl.BlockSpec((1, d), lambda i, j: (0, 0)),
        ] + cast_specs,
        out_specs=out_specs + cast_specs,
        out_shape=out_shape,
        scratch_shapes=[pltpu.VMEM((tm, d), BF16)],
        compiler_params=_params("parallel", "arbitrary"),
        name="ffn",
    )(x, g, w1, w3, w2, gf, *cast_jobs)
    return outs if len(outs) > 1 else outs[0]


def _conv_taps(lc, w_ref):
    return [jnp.broadcast_to(w_ref[lc, j:j + 1, :], (SUBLANES, LANES)) for j in range(CONV_WIDTH)]


def _conv_windows(offsets):
    windows = {}
    for g, off in enumerate(offsets):
        for j in range(CONV_WIDTH):
            windows.setdefault(off + j, []).append((g, j))
    return windows


def _conv_groups(full_ref, lc, base, offsets, taps):
    head = CONV_PAD - CONV_HIST
    accs = [jnp.zeros((SUBLANES, LANES), F32) for _ in offsets]
    for woff, uses in _conv_windows(offsets).items():
        xo = full_ref[lc, pl.ds(base + (head + woff), SUBLANES), :]
        for g, j in uses:
            accs[g] = accs[g] + xo * taps[j]
    return jnp.concatenate(accs, axis=0)


def _ln_silu(cv, b_ref, lng_ref, lnb_ref):
    cv = cv + b_ref[...]
    mu = jnp.mean(cv, axis=-1, keepdims=True)
    var = jnp.mean(jnp.square(cv - mu), axis=-1, keepdims=True)
    return jax.nn.silu((cv - mu) * lax.rsqrt(var + EPS) * lng_ref[...] + lnb_ref[...])


def _inproj_body(*refs, n_heads, d_ret, d_conv, fuse_conv, steps_per_seq, conv_rt):
    if fuse_conv:
        (x_ref, g_ref, w_ref, cos_ref, sin_ref, hist_ref, cw_ref, cb_ref, lng_ref, lnb_ref,
         q_ref, k_ref, v_ref, gate_ref, y_ref, nhist_ref, h_ref, full_ref, cv_ref) = refs
    else:
        x_ref, g_ref, w_ref, cos_ref, sin_ref, q_ref, k_ref, v_ref, gate_ref, u_ref = refs
    tm = x_ref.shape[0]
    n_lc = d_conv // LANES
    head = CONV_PAD - CONV_HIST
    dq = n_heads * RET_DK
    col_a = 2 * dq + 2 * d_ret

    def proj(h, col, width):
        return jnp.dot(h, w_ref[:, col:col + width], preferred_element_type=F32)

    def rotary(t):
        cos, sin = cos_ref[...], sin_ref[...]
        t1, t2 = t[:, :ROPE_HALF], t[:, ROPE_HALF:]
        return jnp.concatenate([t1 * cos - t2 * sin, t2 * cos + t1 * sin], axis=-1)

    def project(h, kind, blk):
        lo = blk * RET_DK
        if kind == "q":
            q_ref[:, lo:lo + RET_DK] = (rotary(proj(h, lo, RET_DK)) * (RET_DK ** -0.5)).astype(BF16)
        elif kind == "k":
            k_ref[:, lo:lo + RET_DK] = rotary(proj(h, dq + lo, RET_DK)).astype(BF16)
        elif kind == "v":
            v_ref[:, lo:lo + RET_DK] = proj(h, 2 * dq + lo, RET_DK).astype(BF16)
        else:
            gate_ref[:, lo:lo + RET_DK] = proj(h, 2 * dq + d_ret + lo, RET_DK).astype(BF16)

    sections = [(kind, blk) for kind, width in (("q", dq), ("k", dq), ("v", d_ret), ("g", d_ret))
                for blk in range(width // RET_DK)]

    if not fuse_conv:
        h = _rms(x_ref[...], g_ref[...]).astype(BF16)
        for kind, blk in sections:
            project(h, kind, blk)
        u_ref[...] = proj(h, col_a, d_conv) * jax.nn.sigmoid(proj(h, col_a + d_conv, d_conv))
        return

    t = pl.program_id(0) % steps_per_seq

    @pl.when(t == 0)
    def _():
        for lc in range(n_lc):
            lanes = slice(lc * LANES, (lc + 1) * LANES)
            full_ref[lc, 0:SUBLANES, :] = jnp.zeros((SUBLANES, LANES), F32)
            full_ref[lc, head:CONV_PAD, :] = hist_ref[:, lanes]

    @pl.when(t > 0)
    def _():
        for lc in range(n_lc):
            full_ref[lc, head:CONV_PAD, :] = full_ref[lc, tm + head:tm + CONV_PAD, :]

    h_ref[...] = _rms(x_ref[...], g_ref[...]).astype(BF16)

    lc_per_blk = RET_DK // LANES
    for c in range(d_conv // RET_DK):
        h = h_ref[...]
        lo = c * RET_DK
        u = proj(h, col_a + lo, RET_DK) * jax.nn.sigmoid(proj(h, col_a + d_conv + lo, RET_DK))
        for s in range(lc_per_blk):
            full_ref[c * lc_per_blk + s, CONV_PAD:CONV_PAD + tm, :] = u[:, s * LANES:(s + 1) * LANES]

    offsets = [r * SUBLANES for r in range(conv_rt // SUBLANES)]
    units = [(r, lc) for r in range(tm // conv_rt) for lc in range(n_lc)]
    per_block = -(-len(units) // len(sections))

    def layer_norm(r):
        rows = slice(r * conv_rt, (r + 1) * conv_rt)
        cv = jnp.concatenate([cv_ref[lc, rows, :] for lc in range(n_lc)], axis=-1)
        y_ref[rows, :] = _ln_silu(cv, cb_ref, lng_ref, lnb_ref).astype(y_ref.dtype)

    done_rows = 0
    normed = 0
    for i, (kind, blk) in enumerate(sections):
        project(h_ref[...], kind, blk)
        for r, lc in units[i * per_block:(i + 1) * per_block]:
            cv_ref[lc, r * conv_rt:(r + 1) * conv_rt, :] = _conv_groups(
                full_ref, lc, r * conv_rt, offsets, _conv_taps(lc, cw_ref))
        for r in range(normed, done_rows):
            layer_norm(r)
        normed = done_rows
        done_rows = min((i + 1) * per_block, len(units)) // n_lc
    for r in range(normed, tm // conv_rt):
        layer_norm(r)
    for lc in range(n_lc):
        nhist_ref[:, lc * LANES:(lc + 1) * LANES] = full_ref[lc, tm + head:tm + CONV_PAD, :]


def _inproj(x, g, w_in, cos, sin, *, n_heads, d_ret, d_conv, tm, conv=None):
    n, d = x.shape
    period = cos.shape[0] // tm
    dq = n_heads * RET_DK
    row = lambda i: (i, 0)
    in_specs = [
        pl.BlockSpec((tm, d), row),
        _resident((1, d)),
        _resident(w_in.shape),
        pl.BlockSpec((tm, ROPE_HALF), lambda i: (i % period, 0)),
        pl.BlockSpec((tm, ROPE_HALF), lambda i: (i % period, 0)),
    ]
    out_specs = [pl.BlockSpec((tm, dq), row), pl.BlockSpec((tm, dq), row),
                 pl.BlockSpec((tm, d_ret), row), pl.BlockSpec((tm, d_ret), row),
                 pl.BlockSpec((tm, d_conv), row)]
    out_shape = [jax.ShapeDtypeStruct((n, dq), BF16), jax.ShapeDtypeStruct((n, dq), BF16),
                 jax.ShapeDtypeStruct((n, d_ret), BF16), jax.ShapeDtypeStruct((n, d_ret), BF16)]
    args = [x, g, w_in, cos, sin]
    scratch = []
    steps_per_seq = conv_rt = None
    if conv is None:
        out_shape.append(jax.ShapeDtypeStruct((n, d_conv), F32))
    else:
        hist, layer, seq, cw, cb, lng, lnb, conv_rt = conv
        assert seq % tm == 0 and tm % conv_rt == 0
        steps_per_seq = seq // tm
        cw = cw.reshape(cw.shape[0], d_conv // LANES, LANES).transpose(1, 0, 2)
        in_specs += [pl.BlockSpec((None, None, CONV_HIST, d_conv), lambda i: (layer, i // steps_per_seq, 0, 0)),
                     _resident(cw.shape), _resident(cb.shape), _resident(lng.shape), _resident(lnb.shape)]
        args += [hist, cw, cb, lng, lnb]
        out_specs.append(pl.BlockSpec((None, None, CONV_HIST, d_conv), lambda i: (0, i // steps_per_seq, 0, 0)))
        out_shape += [jax.ShapeDtypeStruct((n, d_conv), BF16),
                      jax.ShapeDtypeStruct((1, hist.shape[1], CONV_HIST, d_conv), F32)]
        scratch = [pltpu.VMEM((tm, d), BF16),
                   pltpu.VMEM((d_conv // LANES, tm + CONV_PAD, LANES), F32),
                   pltpu.VMEM((d_conv // LANES, tm, LANES), F32)]
    return pl.pallas_call(
        functools.partial(_inproj_body, n_heads=n_heads, d_ret=d_ret, d_conv=d_conv,
                          fuse_conv=conv is not None, steps_per_seq=steps_per_seq, conv_rt=conv_rt),
        grid=(n // tm,),
        in_specs=in_specs,
        out_specs=out_specs,
        out_shape=out_shape,
        scratch_shapes=scratch,
        compiler_params=_params("arbitrary" if conv is not None else "parallel"),
        name="inproj",
    )(*args)


def _retention_body(q_ref, k_ref, v_ref, gate_ref, s0_ref, dmask_ref, qdec_ref, kdec_ref,
                    cdec_ref, gng_ref, gnb_ref, x_ref, cy_ref, w_ref, o_ref, so_ref, y_ref,
                    *ring, n_heads, nb, nc, c, steps, layer, n_groups):
    if steps == 1:
        sbuf, sem = ring
        g = pl.program_id(0)

        def fetch(group, slot):
            return pltpu.make_async_copy(s0_ref.at[layer, pl.ds(group * nb, nb)], sbuf.at[slot], sem.at[slot])

        @pl.when(g == 0)
        def _():
            for first in range(min(RING_SLOTS - 1, n_groups)):
                fetch(first, first).start()

        @pl.when(g + (RING_SLOTS - 1) < n_groups)
        def _():
            ahead = g + (RING_SLOTS - 1)
            fetch(ahead, ahead % RING_SLOTS).start()

        slot = g % RING_SLOTS
        fetch(g, slot).wait()
        state_ref = sbuf.at[slot]
    else:
        state_ref = so_ref

        @pl.when(pl.program_id(1) == 0)
        def _():
            so_ref[...] = s0_ref[...]

    narrow = c % (2 * SUBLANES) != 0
    mm = F32 if narrow else BF16
    nt = (((1,), (1,)), ((), ()))
    tn = (((0,), (0,)), ((), ()))

    for hd in range(n_heads):
        cols = slice(hd * RET_DK, (hd + 1) * RET_DK)
        vcols = slice(hd * RET_DV, (hd + 1) * RET_DV)
        dmask = dmask_ref[hd]
        qdec = qdec_ref[hd]
        kdec = kdec_ref[hd]
        cdec = cdec_ref[hd]
        gng = gng_ref[:, vcols]
        gnb = gnb_ref[:, vcols]
        if narrow:
            q_all = q_ref[:, cols].astype(F32)
            k_all = k_ref[:, cols].astype(F32)
            v_all = v_ref[:, vcols].astype(F32)
            gate_all = gate_ref[:, vcols].astype(F32)
        for bb in range(nb):
            state = state_ref[bb, hd]
            for cc in range(nc):
                rows = slice((bb * nc + cc) * c, (bb * nc + cc + 1) * c)
                if narrow:
                    q, k, v, gate = q_all[rows], k_all[rows], v_all[rows], gate_all[rows]
                else:
                    q, k, v = q_ref[rows, cols], k_ref[rows, cols], v_ref[rows, vcols]
                    gate = gate_ref[rows, vcols].astype(F32)
                s = lax.dot_general(q, k, nt, preferred_element_type=F32) * dmask
                inner = jnp.dot(s.astype(mm), v, preferred_element_type=F32)
                qd = (q.astype(F32) * qdec).astype(mm)
                cross = jnp.dot(qd, state.astype(mm), preferred_element_type=F32)
                o = inner + cross
                kd = (k.astype(F32) * kdec).astype(mm)
                state = state * cdec + lax.dot_general(kd, v, tn, preferred_element_type=F32)
                mu = jnp.mean(o, axis=-1, keepdims=True)
                var = jnp.mean(jnp.square(o - mu), axis=-1, keepdims=True)
                on = (o - mu) * lax.rsqrt(var + EPS) * gng + gnb
                y_ref[rows, vcols] = (jax.nn.silu(gate) * on).astype(y_ref.dtype)
            so_ref[bb, hd] = state

    d_ret = n_heads * RET_DV
    y = jnp.dot(cy_ref[...].astype(BF16), w_ref[d_ret:, :], preferred_element_type=F32)
    y += jnp.dot(y_ref[...].astype(BF16), w_ref[:d_ret, :], preferred_element_type=F32)
    o_ref[...] = x_ref[...] + y


def _retention(q, k, v, gate, state0, gng, gnb, out, *, layer, seq, c, nb, nc):
    n, d_ret = v.shape
    batch, n_heads = state0.shape[1:3]
    steps = seq // (nc * c)
    assert nb == 1 or steps == 1

    lg = jnp.log1p(-jnp.exp2(-5.0 - jnp.arange(n_heads, dtype=F32)))
    idx = jnp.arange(c, dtype=F32)
    diff = idx[:, None] - idx[None, :]
    causal = diff >= 0
    dmask = jnp.where(causal[None], jnp.exp(jnp.where(causal, diff, 0.0)[None] * lg[:, None, None]), 0.0)
    qdec = jnp.broadcast_to(jnp.exp((idx[None, :] + 1.0) * lg[:, None])[:, :, None], (n_heads, c, RET_DK))
    kdec = jnp.broadcast_to(jnp.exp((c - 1.0 - idx[None, :]) * lg[:, None])[:, :, None], (n_heads, c, RET_DK))
    cdec = jnp.broadcast_to(jnp.exp(c * lg)[:, None, None], (n_heads, 1, RET_DV))

    rows = nb * nc * c
    row = lambda b, t: (b * steps + t, 0)
    st_blk = (None, nb) + state0.shape[2:]
    y_dtype = BF16 if c % (2 * SUBLANES) == 0 else F32
    in_specs = [
        pl.BlockSpec((rows, q.shape[1]), row),
        pl.BlockSpec((rows, k.shape[1]), row),
        pl.BlockSpec((rows, d_ret), row),
        pl.BlockSpec((rows, d_ret), row),
        pl.BlockSpec(st_blk, lambda b, t: (layer, b, 0, 0, 0)),
        _resident(dmask.shape),
        _resident(qdec.shape),
        _resident(kdec.shape),
        _resident(cdec.shape),
        _resident(gng.shape),
        _resident(gnb.shape),
    ]
    args = [q, k, v, gate, state0, dmask, qdec, kdec, cdec, gng, gnb]
    state_spec = pl.BlockSpec(st_blk, lambda b, t: (0, b, 0, 0, 0))
    state_shape = jax.ShapeDtypeStruct((1,) + state0.shape[1:], state0.dtype)
    x, conv_y, w_out = out
    in_specs += [pl.BlockSpec((rows, x.shape[1]), row), pl.BlockSpec((rows, conv_y.shape[1]), row),
                 _resident(w_out.shape)]
    args += [x, conv_y, w_out]
    scratch = [pltpu.VMEM((rows, d_ret), y_dtype)]
    if steps == 1:
        in_specs[4] = pl.BlockSpec(memory_space=pl.ANY)
        scratch += [pltpu.VMEM((RING_SLOTS, nb) + state0.shape[2:], state0.dtype),
                    pltpu.SemaphoreType.DMA((RING_SLOTS,))]
    return pl.pallas_call(
        functools.partial(_retention_body, n_heads=n_heads, nb=nb, nc=nc, c=c, steps=steps,
                          layer=layer, n_groups=batch // nb),
        grid=(batch // nb, steps),
        in_specs=in_specs,
        out_specs=[pl.BlockSpec((rows, x.shape[1]), row), state_spec],
        out_shape=[jax.ShapeDtypeStruct(x.shape, F32), state_shape],
        scratch_shapes=scratch,
        compiler_params=_params("arbitrary" if steps == 1 else "parallel", "arbitrary"),
        name="retention_outproj",
    )(*args)


def _conv_body(u_ref, hist_ref, w_ref, b_ref, lng_ref, lnb_ref, y_ref, nhist_ref, full_ref, cv_ref,
               stage_ref, *, bb, seq):
    d = u_ref.shape[-1]
    n_lc = d // LANES

    for lc in range(n_lc):
        lanes = slice(lc * LANES, (lc + 1) * LANES)
        for tau in range(CONV_HIST):
            full_ref[lc, tau] = hist_ref[tau, :, lanes]
        stage_ref[lc] = u_ref[:, lanes]
        for t in range(seq):
            full_ref[lc, CONV_HIST + t] = stage_ref[lc, pl.ds(t, bb, stride=seq), :]

    def conv_lanes(lc, carry):
        taps = [jnp.broadcast_to(w_ref[lc, j:j + 1, :], (bb, LANES)) for j in range(CONV_WIDTH)]
        accs = [jnp.zeros((bb, LANES), F32) for _ in range(seq)]
        for tau in range(CONV_HIST + seq):
            xo = full_ref[lc, tau]
            for t in range(seq):
                if 0 <= tau - t < CONV_WIDTH:
                    accs[t] = accs[t] + xo * taps[tau - t]
        for t in range(seq):
            cv_ref[lc, t] = accs[t]
        return carry

    lax.fori_loop(0, n_lc, conv_lanes, 0)
    cv = jnp.concatenate([cv_ref[lc].reshape(seq * bb, LANES) for lc in range(n_lc)], axis=-1)
    y = _ln_silu(cv, b_ref, lng_ref, lnb_ref)
    for lc in range(n_lc):
        lanes = slice(lc * LANES, (lc + 1) * LANES)
        for t in range(seq):
            stage_ref[lc, pl.ds(t, bb, stride=seq), :] = y[t * bb:(t + 1) * bb, lanes]
        y_ref[:, lanes] = stage_ref[lc]
        for tau in range(CONV_HIST):
            nhist_ref[tau, :, lanes] = full_ref[lc, seq + tau]


def _conv(u, hist, w, b, lng, lnb, *, layer, seq, bb):
    n, d = u.shape
    batch = n // seq
    assert seq <= CONV_HIST and bb % SUBLANES == 0
    w = w.reshape(w.shape[0], d // LANES, LANES).transpose(1, 0, 2)
    row = lambda i: (i, 0)
    return pl.pallas_call(
        functools.partial(_conv_body, bb=bb, seq=seq),
        grid=(batch // bb,),
        in_specs=[
            pl.BlockSpec((bb * seq, d), row),
            pl.BlockSpec((None, CONV_HIST, bb, d), lambda i: (layer, 0, i, 0)),
            _resident(w.shape),
            _resident(b.shape),
            _resident(lng.shape),
            _resident(lnb.shape),
        ],
        out_specs=[
            pl.BlockSpec((bb * seq, d), row),
            pl.BlockSpec((None, CONV_HIST, bb, d), lambda i: (0, 0, i, 0)),
        ],
        out_shape=[
            jax.ShapeDtypeStruct((n, d), F32),
            jax.ShapeDtypeStruct((1, CONV_HIST, batch, d), u.dtype),
        ],
        scratch_shapes=[pltpu.VMEM((d // LANES, CONV_HIST + seq, bb, LANES), F32),
                        pltpu.VMEM((d // LANES, seq, bb, LANES), F32),
                        pltpu.VMEM((d // LANES, bb * seq, LANES), F32)],
        compiler_params=_params("parallel"),
        name="conv",
    )(u, hist, w, b, lng, lnb)


def _rope_tables(pos, rows):
    inv_freq = ROPE_BASE ** (-jnp.arange(ROPE_HALF, dtype=F32) / ROPE_HALF)
    ang = pos[:, None] * inv_freq[None, :]
    reps = max(1, rows // pos.shape[0])
    return jnp.tile(jnp.cos(ang), (reps, 1)), jnp.tile(jnp.sin(ang), (reps, 1))


def _stream(x, pos, ret_state, conv_buf, wts, ffn1_w, ffn2_w, *, layer, emit_bf16, ffn_tm, tf, tm,
            ret_nb, ret_nc, conv_bb, conv_rt):
    batch, seq, d = x.shape
    n_heads = ret_state.shape[2]
    d_ret = n_heads * RET_DV
    d_conv = conv_buf.shape[-1]
    c = min(seq, RET_CHUNK)
    xf = x.reshape(batch * seq, d)

    x1 = _ffn(xf, wts["norm_ffn1_g"], *ffn1_w, wts["norm_final_g"], final_norm=False,
              tm=ffn_tm, tf=tf, emit_bf16=emit_bf16,
              cast_jobs=(wts["w_in"], wts["w_out"]) if emit_bf16 else ())
    if emit_bf16:
        x1, *ffn1_w = x1
        *ffn1_w, w_in, w_out = ffn1_w
        wts = dict(wts, w_in=w_in, w_out=w_out)
    cos, sin = _rope_tables(pos, tm)
    conv_w = (wts["conv_w"], wts["conv_b"], wts["conv_ln_g"], wts["conv_ln_b"])
    fuse_conv = seq % tm == 0
    q, k, v, gate, *conv_out = _inproj(
        x1, wts["norm_mix_g"], wts["w_in"], cos, sin, n_heads=n_heads, d_ret=d_ret, d_conv=d_conv, tm=tm,
        conv=(conv_buf, layer, seq, *conv_w, conv_rt) if fuse_conv else None)
    if fuse_conv:
        conv_y, new_buf = conv_out
    else:
        conv_y, new_buf = _conv(conv_out[0], conv_buf.transpose(0, 2, 1, 3), *conv_w,
                                layer=layer, seq=seq, bb=conv_bb)
        new_buf = new_buf.transpose(0, 2, 1, 3)
    x2, new_ret = _retention(q, k, v, gate, ret_state, wts["ret_gn_g"], wts["ret_gn_b"],
                             (x1, conv_y, wts["w_out"]), layer=layer, seq=seq, c=c, nb=ret_nb, nc=ret_nc)
    y = _ffn(x2, wts["norm_ffn2_g"], *ffn2_w, wts["norm_final_g"], final_norm=True,
             tm=ffn_tm, tf=tf, emit_bf16=emit_bf16)
    if emit_bf16:
        y, *ffn2_w = y
    return y.reshape(batch, seq, d), new_ret, new_buf, wts, tuple(ffn1_w), tuple(ffn2_w)


def kernel(x_prompt, x_sample, state_retention, state_conv, norm_ffn1_g, ffn1_w1, ffn1_w3, ffn1_w2, norm_mix_g, w_in, ret_gn_g, ret_gn_b, conv_w, conv_b, conv_ln_g, conv_ln_b, w_out, norm_ffn2_g, ffn2_w1, ffn2_w3, ffn2_w2, norm_final_g):
    depth = w_in.shape[0]
    assert depth == 1
    layer = 0
    row = lambda g: g.reshape(1, -1)
    wts = {
        "norm_ffn1_g": row(norm_ffn1_g[layer]),
        "norm_mix_g": row(norm_mix_g[layer]),
        "w_in": w_in[layer],
        "ret_gn_g": row(ret_gn_g[layer]),
        "ret_gn_b": row(ret_gn_b[layer]),
        "conv_w": conv_w[layer],
        "conv_b": row(conv_b[layer]),
        "conv_ln_g": row(conv_ln_g[layer]),
        "conv_ln_b": row(conv_ln_b[layer]),
        "w_out": w_out[layer],
        "norm_ffn2_g": row(norm_ffn2_g[layer]),
        "norm_final_g": row(norm_final_g),
    }
    bp, tp, _ = x_prompt.shape
    bs, ts, _ = x_sample.shape
    n_heads = state_retention.shape[2]
    d_conv = state_conv.shape[-1]

    pos_p = jnp.arange(tp, dtype=F32)
    pos_s = PAST_LEN + jnp.arange(ts, dtype=F32)
    r0 = jnp.zeros((1, bp, n_heads, RET_DK, RET_DV), state_retention.dtype)
    c0 = jnp.zeros((1, bp, CONV_HIST, d_conv), x_prompt.dtype)

    ffn1_w = (ffn1_w1[layer], ffn1_w3[layer], ffn1_w2[layer])
    ffn2_w = (ffn2_w1[layer], ffn2_w3[layer], ffn2_w2[layer])
    y_s, ret_s, conv_s, wts, ffn1_w, ffn2_w = _stream(
        x_sample, pos_s, state_retention, state_conv, wts, ffn1_w, ffn2_w, layer=layer,
        emit_bf16=True, ffn_tm=1024, tf=256, tm=512, ret_nb=8, ret_nc=1, conv_bb=16, conv_rt=None)
    y_p, ret_p, conv_p, _, _, _ = _stream(
        x_prompt, pos_p, r0, c0, wts, ffn1_w, ffn2_w, layer=0,
        emit_bf16=False, ffn_tm=1024, tf=512, tm=512, ret_nb=1, ret_nc=4, conv_bb=None, conv_rt=128)
    return (y_p, y_s, ret_p, conv_p, ret_s, conv_s)
```

```python
import functools

import jax
import jax.numpy as jnp
from jax import lax
from jax.experimental import pallas as pl
from jax.experimental.pallas import tpu as pltpu

F32 = jnp.float32
BF16 = jnp.bfloat16

RET_DK = 256
RET_DV = 256
RET_CHUNK = 128
CONV_WIDTH = 31
ROPE_BASE = 10000.0
EPS = 1e-6
PAST_LEN = 16384

LANES = 128
SUBLANES = 8
VMEM_LIMIT_BYTES = 60 * 1024 * 1024

ROPE_HALF = RET_DK // 2
CONV_HIST = CONV_WIDTH - 1
CONV_PAD = 32
ROW_CHUNK = 128
CAST_ROWS = 128


def _rms(x, g):
    return x * lax.rsqrt(jnp.mean(x * x, axis=-1, keepdims=True) + EPS) * g


def _params(*semantics):
    return pltpu.CompilerParams(dimension_semantics=semantics,
                                vmem_limit_bytes=VMEM_LIMIT_BYTES)


def _resident(shape):
    nd = len(shape)
    return pl.BlockSpec(shape, lambda *_: (0,) * nd, pipeline_mode=pl.Buffered(1))


def _ffn_body(x_ref, g_ref, w1_ref, w3_ref, w2_ref, gf_ref, *rest, final_norm, emit_bf16, n_cast):
    n_emit = 3 if emit_bf16 else 0
    cast_src = rest[:n_cast]
    o_ref = rest[n_cast]
    emit_refs = rest[n_cast + 1:n_cast + 1 + n_emit]
    cast_dst = rest[n_cast + 1 + n_emit:n_cast + 1 + n_emit + n_cast]
    h_ref = rest[-1]
    j = pl.program_id(1)

    def step(first, last):
        if first:
            h = _rms(x_ref[...], g_ref[...]).astype(BF16)
            h_ref[...] = h
        else:
            h = h_ref[...]
        w1, w3, w2 = (w_ref[...].astype(BF16) for w_ref in (w1_ref, w3_ref, w2_ref))
        for dst_ref, w in zip(emit_refs, (w1, w3, w2)):
            dst_ref[...] = w
        for src_ref, dst_ref in zip(cast_src, cast_dst):
            dst_ref[...] = src_ref[...].astype(BF16)
        a = jnp.dot(h, w1, preferred_element_type=F32)
        b = jnp.dot(h, w3, preferred_element_type=F32)
        act = (jax.nn.silu(a) * b).astype(BF16)
        part = jnp.dot(act, w2, preferred_element_type=F32)
        if first:
            o_ref[...] = part
        elif last and not final_norm:
            o_ref[...] = x_ref[...] + 0.5 * (o_ref[...] + part)
        else:
            o_ref[...] += part

    n_steps = pl.num_programs(1)

    @pl.when(j == 0)
    def _():
        step(True, False)

    @pl.when(jnp.logical_and(j > 0, j < n_steps - 1))
    def _():
        step(False, False)

    @pl.when(j == n_steps - 1)
    def _():
        step(False, True)
        if final_norm:
            def finish(r, carry):
                rows = pl.ds(pl.multiple_of(r * ROW_CHUNK, ROW_CHUNK), ROW_CHUNK)
                o_ref[rows, :] = _rms(x_ref[rows, :] + 0.5 * o_ref[rows, :], gf_ref[...])
                return carry
            lax.fori_loop(0, x_ref.shape[0] // ROW_CHUNK, finish, 0)


def _ffn(x, g, w1, w3, w2, gf, *, final_norm, tm, tf, emit_bf16=False, cast_jobs=()):
    n, d = x.shape
    f = w1.shape[1]
    steps = f // tf
    assert steps >= 2
    up = pl.BlockSpec((d, tf), lambda i, j: (0, j))
    down = pl.BlockSpec((tf, d), lambda i, j: (j, 0))
    out_specs = [pl.BlockSpec((tm, d), lambda i, j: (i, 0))]
    out_shape = [jax.ShapeDtypeStruct((n, d), F32)]
    if emit_bf16:
        out_specs += [up, up, down]
        out_shape += [jax.ShapeDtypeStruct(w.shape, BF16) for w in (w1, w3, w2)]
    cast_specs = []
    for m in cast_jobs:
        blocks = min(steps, m.shape[0] // CAST_ROWS)
        rows = m.shape[0] // blocks
        assert rows * blocks == m.shape[0] and rows % (2 * SUBLANES) == 0
        cast_specs.append(pl.BlockSpec((rows, m.shape[1]), lambda i, j, nb=blocks: (jnp.minimum(j, nb - 1), 0)))
        out_shape.append(jax.ShapeDtypeStruct(m.shape, BF16))
    outs = pl.pallas_call(
        functools.partial(_ffn_body, final_norm=final_norm, emit_bf16=emit_bf16, n_cast=len(cast_jobs)),
        grid=(n // tm, steps),
        in_specs=[
            pl.BlockSpec((tm, d), lambda i, j: (i, 0)),
            pl.BlockSpec((1, d), lambda i, j: (0, 0)),
            up,
            up,
            down,
            pl.BlockSpec((1, d), lambda i, j: (0, 0)),
        ] + cast_specs,
        out_specs=out_specs + cast_specs,
        out_shape=out_shape,
        scratch_shapes=[pltpu.VMEM((tm, d), BF16)],
        compiler_params=_params("parallel", "arbitrary"),
        name="ffn",
    )(x, g, w1, w3, w2, gf, *cast_jobs)
    return outs if len(outs) > 1 else outs[0]


def _conv_taps(lc, w_ref):
    return [jnp.broadcast_to(w_ref[lc, j:j + 1, :], (SUBLANES, LANES)) for j in range(CONV_WIDTH)]


def _conv_windows(offsets):
    windows = {}
    for g, off in enumerate(offsets):
        for j in range(CONV_WIDTH):
            windows.setdefault(off + j, []).append((g, j))
    return windows


def _conv_groups(full_ref, lc, base, offsets, taps):
    head = CONV_PAD - CONV_HIST
    accs = [jnp.zeros((SUBLANES, LANES), F32) for _ in offsets]
    for woff, uses in _conv_windows(offsets).items():
        xo = full_ref[lc, pl.ds(base + (head + woff), SUBLANES), :]
        for g, j in uses:
            accs[g] = accs[g] + xo * taps[j]
    return jnp.concatenate(accs, axis=0)


def _ln_silu(cv, b_ref, lng_ref, lnb_ref):
    cv = cv + b_ref[...]
    mu = jnp.mean(cv, axis=-1, keepdims=True)
    var = jnp.mean(jnp.square(cv - mu), axis=-1, keepdims=True)
    return jax.nn.silu((cv - mu) * lax.rsqrt(var + EPS) * lng_ref[...] + lnb_ref[...])


def _inproj_body(*refs, n_heads, d_ret, d_conv, fuse_conv, steps_per_seq, conv_rt):
    if fuse_conv:
        (x_ref, g_ref, w_ref, cos_ref, sin_ref, hist_ref, cw_ref, cb_ref, lng_ref, lnb_ref,
         q_ref, k_ref, v_ref, gate_ref, y_ref, nhist_ref, h_ref, full_ref, cv_ref) = refs
    else:
        x_ref, g_ref, w_ref, cos_ref, sin_ref, q_ref, k_ref, v_ref, gate_ref, u_ref = refs
    tm = x_ref.shape[0]
    n_lc = d_conv // LANES
    head = CONV_PAD - CONV_HIST
    dq = n_heads * RET_DK
    col_a = 2 * dq + 2 * d_ret

    def proj(h, col, width):
        return jnp.dot(h, w_ref[:, col:col + width], preferred_element_type=F32)

    def rotary(t):
        cos, sin = cos_ref[...], sin_ref[...]
        t1, t2 = t[:, :ROPE_HALF], t[:, ROPE_HALF:]
        return jnp.concatenate([t1 * cos - t2 * sin, t2 * cos + t1 * sin], axis=-1)

    def project(h, kind, blk):
        lo = blk * RET_DK
        if kind == "q":
            q_ref[:, lo:lo + RET_DK] = (rotary(proj(h, lo, RET_DK)) * (RET_DK ** -0.5)).astype(BF16)
        elif kind == "k":
            k_ref[:, lo:lo + RET_DK] = rotary(proj(h, dq + lo, RET_DK)).astype(BF16)
        elif kind == "v":
            v_ref[:, lo:lo + RET_DK] = proj(h, 2 * dq + lo, RET_DK).astype(BF16)
        else:
            gate_ref[:, lo:lo + RET_DK] = proj(h, 2 * dq + d_ret + lo, RET_DK).astype(BF16)

    sections = [(kind, blk) for kind, width in (("q", dq), ("k", dq), ("v", d_ret), ("g", d_ret))
                for blk in range(width // RET_DK)]

    if not fuse_conv:
        h = _rms(x_ref[...], g_ref[...]).astype(BF16)
        for kind, blk in sections:
            project(h, kind, blk)
        u_ref[...] = proj(h, col_a, d_conv) * jax.nn.sigmoid(proj(h, col_a + d_conv, d_conv))
        return

    t = pl.program_id(0) % steps_per_seq

    @pl.when(t == 0)
    def _():
        for lc in range(n_lc):
            lanes = slice(lc * LANES, (lc + 1) * LANES)
            full_ref[lc, 0:SUBLANES, :] = jnp.zeros((SUBLANES, LANES), F32)
            full_ref[lc, head:CONV_PAD, :] = hist_ref[:, lanes]

    @pl.when(t > 0)
    def _():
        for lc in range(n_lc):
            full_ref[lc, head:CONV_PAD, :] = full_ref[lc, tm + head:tm + CONV_PAD, :]

    h_ref[...] = _rms(x_ref[...], g_ref[...]).astype(BF16)

    lc_per_blk = RET_DK // LANES
    for c in range(d_conv // RET_DK):
        h = h_ref[...]
        lo = c * RET_DK
        u = proj(h, col_a + lo, RET_DK) * jax.nn.sigmoid(proj(h, col_a + d_conv + lo, RET_DK))
        for s in range(lc_per_blk):
            full_ref[c * lc_per_blk + s, CONV_PAD:CONV_PAD + tm, :] = u[:, s * LANES:(s + 1) * LANES]

    offsets = [r * SUBLANES for r in range(conv_rt // SUBLANES)]
    units = [(r, lc) for r in range(tm // conv_rt) for lc in range(n_lc)]
    per_block = -(-len(units) // len(sections))

    def layer_norm(r):
        rows = slice(r * conv_rt, (r + 1) * conv_rt)
        cv = jnp.concatenate([cv_ref[lc, rows, :] for lc in range(n_lc)], axis=-1)
        y_ref[rows, :] = _ln_silu(cv, cb_ref, lng_ref, lnb_ref).astype(y_ref.dtype)

    done_rows = 0
    normed = 0
    for i, (kind, blk) in enumerate(sections):
        project(h_ref[...], kind, blk)
        for r, lc in units[i * per_block:(i + 1) * per_block]:
            cv_ref[lc, r * conv_rt:(r + 1) * conv_rt, :] = _conv_groups(
                full_ref, lc, r * conv_rt, offsets, _conv_taps(lc, cw_ref))
        for r in range(normed, done_rows):
            layer_norm(r)
        normed = done_rows
        done_rows = min((i + 1) * per_block, len(units)) // n_lc
    for r in range(normed, tm // conv_rt):
        layer_norm(r)
    for lc in range(n_lc):
        nhist_ref[:, lc * LANES:(lc + 1) * LANES] = full_ref[lc, tm + head:tm + CONV_PAD, :]


def _inproj(x, g, w_in, cos, sin, *, n_heads, d_ret, d_conv, tm, conv=None):
    n, d = x.shape
    period = cos.shape[0] // tm
    dq = n_heads * RET_DK
    row = lambda i: (i, 0)
    in_specs = [
        pl.BlockSpec((tm, d), row),
        _resident((1, d)),
        _resident(w_in.shape),
        pl.BlockSpec((tm, ROPE_HALF), lambda i: (i % period, 0)),
        pl.BlockSpec((tm, ROPE_HALF), lambda i: (i % period, 0)),
    ]
    out_specs = [pl.BlockSpec((tm, dq), row), pl.BlockSpec((tm, dq), row),
                 pl.BlockSpec((tm, d_ret), row), pl.BlockSpec((tm, d_ret), row),
                 pl.BlockSpec((tm, d_conv), row)]
    out_shape = [jax.ShapeDtypeStruct((n, dq), BF16), jax.ShapeDtypeStruct((n, dq), BF16),
                 jax.ShapeDtypeStruct((n, d_ret), BF16), jax.ShapeDtypeStruct((n, d_ret), BF16)]
    args = [x, g, w_in, cos, sin]
    scratch = []
    steps_per_seq = conv_rt = None
    if conv is None:
        out_shape.append(jax.ShapeDtypeStruct((n, d_conv), F32))
    else:
        hist, layer, seq, cw, cb, lng, lnb, conv_rt = conv
        assert seq % tm == 0 and tm % conv_rt == 0
        steps_per_seq = seq // tm
        cw = cw.reshape(cw.shape[0], d_conv // LANES, LANES).transpose(1, 0, 2)
        in_specs += [pl.BlockSpec((None, None, CONV_HIST, d_conv), lambda i: (layer, i // steps_per_seq, 0, 0)),
                     _resident(cw.shape), _resident(cb.shape), _resident(lng.shape), _resident(lnb.shape)]
        args += [hist, cw, cb, lng, lnb]
        out_specs.append(pl.BlockSpec((None, None, CONV_HIST, d_conv), lambda i: (0, i // steps_per_seq, 0, 0)))
        out_shape += [jax.ShapeDtypeStruct((n, d_conv), BF16),
                      jax.ShapeDtypeStruct((1, hist.shape[1], CONV_HIST, d_conv), F32)]
        scratch = [pltpu.VMEM((tm, d), BF16),
                   pltpu.VMEM((d_conv // LANES, tm + CONV_PAD, LANES), F32),
                   pltpu.VMEM((d_conv // LANES, tm, LANES), F32)]
    return pl.pallas_call(
        functools.partial(_inproj_body, n_heads=n_heads, d_ret=d_ret, d_conv=d_conv,
                          fuse_conv=conv is not None, steps_per_seq=steps_per_seq, conv_rt=conv_rt),
        grid=(n // tm,),
        in_specs=in_specs,
        out_specs=out_specs,
        out_shape=out_shape,
        scratch_shapes=scratch,
        compiler_params=_params("arbitrary" if conv is not None else "parallel"),
        name="inproj",
    )(*args)


def _retention_body(q_ref, k_ref, v_ref, gate_ref, s0_ref, dmask_ref, qdec_ref, kdec_ref,
                    cdec_ref, gng_ref, gnb_ref, x_ref, cy_ref, w_ref, o_ref, so_ref, y_ref,
                    *, n_heads, nb, nc, c, steps):
    if steps == 1:
        state_ref = s0_ref
    else:
        state_ref = so_ref

        @pl.when(pl.program_id(1) == 0)
        def _():
            so_ref[...] = s0_ref[...]

    narrow = c % (2 * SUBLANES) != 0
    mm = F32 if narrow else BF16
    nt = (((1,), (1,)), ((), ()))
    tn = (((0,), (0,)), ((), ()))

    for hd in range(n_heads):
        cols = slice(hd * RET_DK, (hd + 1) * RET_DK)
        vcols = slice(hd * RET_DV, (hd + 1) * RET_DV)
        dmask = dmask_ref[hd]
        qdec = qdec_ref[hd]
        kdec = kdec_ref[hd]
        cdec = cdec_ref[hd]
        gng = gng_ref[:, vcols]
        gnb = gnb_ref[:, vcols]
        if narrow:
            q_all = q_ref[:, cols].astype(F32)
            k_all = k_ref[:, cols].astype(F32)
            v_all = v_ref[:, vcols].astype(F32)
            gate_all = gate_ref[:, vcols].astype(F32)
        for bb in range(nb):
            state = state_ref[bb, hd]
            for cc in range(nc):
                rows = slice((bb * nc + cc) * c, (bb * nc + cc + 1) * c)
                if narrow:
                    q, k, v, gate = q_all[rows], k_all[rows], v_all[rows], gate_all[rows]
                else:
                    q, k, v = q_ref[rows, cols], k_ref[rows, cols], v_ref[rows, vcols]
                    gate = gate_ref[rows, vcols].astype(F32)
                s = lax.dot_general(q, k, nt, preferred_element_type=F32) * dmask
                inner = jnp.dot(s.astype(mm), v, preferred_element_type=F32)
                qd = (q.astype(F32) * qdec).astype(mm)
                cross = jnp.dot(qd, state.astype(mm), preferred_element_type=F32)
                o = inner + cross
                kd = (k.astype(F32) * kdec).astype(mm)
                state = state * cdec + lax.dot_general(kd, v, tn, preferred_element_type=F32)
                mu = jnp.mean(o, axis=-1, keepdims=True)
                var = jnp.mean(jnp.square(o - mu), axis=-1, keepdims=True)
                on = (o - mu) * lax.rsqrt(var + EPS) * gng + gnb
                y_ref[rows, vcols] = (jax.nn.silu(gate) * on).astype(y_ref.dtype)
            so_ref[bb, hd] = state

    d_ret = n_heads * RET_DV
    y = jnp.dot(cy_ref[...].astype(BF16), w_ref[d_ret:, :], preferred_element_type=F32)
    y += jnp.dot(y_ref[...].astype(BF16), w_ref[:d_ret, :], preferred_element_type=F32)
    o_ref[...] = x_ref[...] + y


def _retention(q, k, v, gate, state0, gng, gnb, out, *, layer, seq, c, nb, nc):
    n, d_ret = v.shape
    batch, n_heads = state0.shape[1:3]
    steps = seq // (nc * c)
    assert nb == 1 or steps == 1

    lg = jnp.log1p(-jnp.exp2(-5.0 - jnp.arange(n_heads, dtype=F32)))
    idx = jnp.arange(c, dtype=F32)
    diff = idx[:, None] - idx[None, :]
    causal = diff >= 0
    dmask = jnp.where(causal[None], jnp.exp(jnp.where(causal, diff, 0.0)[None] * lg[:, None, None]), 0.0)
    qdec = jnp.broadcast_to(jnp.exp((idx[None, :] + 1.0) * lg[:, None])[:, :, None], (n_heads, c, RET_DK))
    kdec = jnp.broadcast_to(jnp.exp((c - 1.0 - idx[None, :]) * lg[:, None])[:, :, None], (n_heads, c, RET_DK))
    cdec = jnp.broadcast_to(jnp.exp(c * lg)[:, None, None], (n_heads, 1, RET_DV))

    rows = nb * nc * c
    row = lambda b, t: (b * steps + t, 0)
    st_blk = (None, nb) + state0.shape[2:]
    y_dtype = BF16 if c % (2 * SUBLANES) == 0 else F32
    in_specs = [
        pl.BlockSpec((rows, q.shape[1]), row),
        pl.BlockSpec((rows, k.shape[1]), row),
        pl.BlockSpec((rows, d_ret), row),
        pl.BlockSpec((rows, d_ret), row),
        pl.BlockSpec(st_blk, lambda b, t: (layer, b, 0, 0, 0)),
        _resident(dmask.shape),
        _resident(qdec.shape),
        _resident(kdec.shape),
        _resident(cdec.shape),
        _resident(gng.shape),
        _resident(gnb.shape),
    ]
    args = [q, k, v, gate, state0, dmask, qdec, kdec, cdec, gng, gnb]
    state_spec = pl.BlockSpec(st_blk, lambda b, t: (0, b, 0, 0, 0))
    state_shape = jax.ShapeDtypeStruct((1,) + state0.shape[1:], state0.dtype)
    x, conv_y, w_out = out
    in_specs += [pl.BlockSpec((rows, x.shape[1]), row), pl.BlockSpec((rows, conv_y.shape[1]), row),
                 _resident(w_out.shape)]
    args += [x, conv_y, w_out]
    return pl.pallas_call(
        functools.partial(_retention_body, n_heads=n_heads, nb=nb, nc=nc, c=c, steps=steps),
        grid=(batch // nb, steps),
        in_specs=in_specs,
        out_specs=[pl.BlockSpec((rows, x.shape[1]), row), state_spec],
        out_shape=[jax.ShapeDtypeStruct(x.shape, F32), state_shape],
        scratch_shapes=[pltpu.VMEM((rows, d_ret), y_dtype)],
        compiler_params=_params("parallel", "arbitrary"),
        name="retention_outproj",
    )(*args)


def _conv_body(u_ref, hist_ref, w_ref, b_ref, lng_ref, lnb_ref, *rest, bb, seq, cast):
    if cast:
        src_ref, y_ref, nhist_ref, dst_ref, full_ref, cv_ref, stage_ref = rest
        dst_ref[...] = src_ref[...].astype(BF16)
    else:
        y_ref, nhist_ref, full_ref, cv_ref, stage_ref = rest
    d = u_ref.shape[-1]
    n_lc = d // LANES

    for lc in range(n_lc):
        lanes = slice(lc * LANES, (lc + 1) * LANES)
        for tau in range(CONV_HIST):
            full_ref[lc, tau] = hist_ref[tau, :, lanes]
        stage_ref[lc] = u_ref[:, lanes]
        for t in range(seq):
            full_ref[lc, CONV_HIST + t] = stage_ref[lc, pl.ds(t, bb, stride=seq), :]

    def conv_lanes(lc, carry):
        taps = [jnp.broadcast_to(w_ref[lc, j:j + 1, :], (bb, LANES)) for j in range(CONV_WIDTH)]
        accs = [jnp.zeros((bb, LANES), F32) for _ in range(seq)]
        for tau in range(CONV_HIST + seq):
            xo = full_ref[lc, tau]
            for t in range(seq):
                if 0 <= tau - t < CONV_WIDTH:
                    accs[t] = accs[t] + xo * taps[tau - t]
        for t in range(seq):
            cv_ref[lc, t] = accs[t]
        return carry

    lax.fori_loop(0, n_lc, conv_lanes, 0)
    cv = jnp.concatenate([cv_ref[lc].reshape(seq * bb, LANES) for lc in range(n_lc)], axis=-1)
    y = _ln_silu(cv, b_ref, lng_ref, lnb_ref)
    for lc in range(n_lc):
        lanes = slice(lc * LANES, (lc + 1) * LANES)
        for t in range(seq):
            stage_ref[lc, pl.ds(t, bb, stride=seq), :] = y[t * bb:(t + 1) * bb, lanes]
        y_ref[:, lanes] = stage_ref[lc]
        for tau in range(CONV_HIST):
            nhist_ref[tau, :, lanes] = full_ref[lc, seq + tau]


def _conv(u, hist, w, b, lng, lnb, *, layer, seq, bb, cast_job=None):
    n, d = u.shape
    batch = n // seq
    steps = batch // bb
    assert seq <= CONV_HIST and bb % SUBLANES == 0
    w = w.reshape(w.shape[0], d // LANES, LANES).transpose(1, 0, 2)
    row = lambda i: (i, 0)
    in_specs = [
        pl.BlockSpec((bb * seq, d), row),
        pl.BlockSpec((None, CONV_HIST, bb, d), lambda i: (layer, 0, i, 0)),
        _resident(w.shape),
        _resident(b.shape),
        _resident(lng.shape),
        _resident(lnb.shape),
    ]
    out_specs = [
        pl.BlockSpec((bb * seq, d), row),
        pl.BlockSpec((None, CONV_HIST, bb, d), lambda i: (0, 0, i, 0)),
    ]
    out_shape = [
        jax.ShapeDtypeStruct((n, d), F32),
        jax.ShapeDtypeStruct((1, CONV_HIST, batch, d), u.dtype),
    ]
    args = [u, hist, w, b, lng, lnb]
    if cast_job is not None:
        rows = cast_job.shape[0] // steps
        assert rows * steps == cast_job.shape[0] and rows % (2 * SUBLANES) == 0
        cast_spec = pl.BlockSpec((rows, cast_job.shape[1]), row)
        in_specs.append(cast_spec)
        out_specs.append(cast_spec)
        out_shape.append(jax.ShapeDtypeStruct(cast_job.shape, BF16))
        args.append(cast_job)
    return pl.pallas_call(
        functools.partial(_conv_body, bb=bb, seq=seq, cast=cast_job is not None),
        grid=(steps,),
        in_specs=in_specs,
        out_specs=out_specs,
        out_shape=out_shape,
        scratch_shapes=[pltpu.VMEM((d // LANES, CONV_HIST + seq, bb, LANES), F32),
                        pltpu.VMEM((d // LANES, seq, bb, LANES), F32),
                        pltpu.VMEM((d // LANES, bb * seq, LANES), F32)],
        compiler_params=_params("parallel"),
        name="conv",
    )(*args)


def _rope_tables(pos, rows):
    inv_freq = ROPE_BASE ** (-jnp.arange(ROPE_HALF, dtype=F32) / ROPE_HALF)
    ang = pos[:, None] * inv_freq[None, :]
    reps = max(1, rows // pos.shape[0])
    return jnp.tile(jnp.cos(ang), (reps, 1)), jnp.tile(jnp.sin(ang), (reps, 1))


def _stream(x, pos, ret_state, conv_buf, wts, ffn1_w, ffn2_w, *, layer, emit_bf16, ffn_tm, tf, tm,
            ret_nb, ret_nc, conv_bb, conv_rt):
    batch, seq, d = x.shape
    n_heads = ret_state.shape[2]
    d_ret = n_heads * RET_DV
    d_conv = conv_buf.shape[-1]
    c = min(seq, RET_CHUNK)
    xf = x.reshape(batch * seq, d)

    x1 = _ffn(xf, wts["norm_ffn1_g"], *ffn1_w, wts["norm_final_g"], final_norm=False,
              tm=ffn_tm, tf=tf, emit_bf16=emit_bf16,
              cast_jobs=(wts["w_in"],) if emit_bf16 else ())
    if emit_bf16:
        x1, *ffn1_w, w_in = x1
        wts = dict(wts, w_in=w_in)
    cos, sin = _rope_tables(pos, tm)
    conv_w = (wts["conv_w"], wts["conv_b"], wts["conv_ln_g"], wts["conv_ln_b"])
    fuse_conv = seq % tm == 0
    q, k, v, gate, *conv_out = _inproj(
        x1, wts["norm_mix_g"], wts["w_in"], cos, sin, n_heads=n_heads, d_ret=d_ret, d_conv=d_conv, tm=tm,
        conv=(conv_buf, layer, seq, *conv_w, conv_rt) if fuse_conv else None)
    if fuse_conv:
        conv_y, new_buf = conv_out
    else:
        cast_job = wts["w_out"] if wts["w_out"].dtype != BF16 else None
        conv_y, new_buf, *w_out = _conv(conv_out[0], conv_buf.transpose(0, 2, 1, 3), *conv_w,
                                        layer=layer, seq=seq, bb=conv_bb, cast_job=cast_job)
        new_buf = new_buf.transpose(0, 2, 1, 3)
        if w_out:
            wts = dict(wts, w_out=w_out[0])
    assert wts["w_out"].dtype == BF16
    x2, new_ret = _retention(q, k, v, gate, ret_state, wts["ret_gn_g"], wts["ret_gn_b"],
                             (x1, conv_y, wts["w_out"]), layer=layer, seq=seq, c=c, nb=ret_nb, nc=ret_nc)
    y = _ffn(x2, wts["norm_ffn2_g"], *ffn2_w, wts["norm_final_g"], final_norm=True,
             tm=ffn_tm, tf=tf, emit_bf16=emit_bf16)
    if emit_bf16:
        y, *ffn2_w = y
    return y.reshape(batch, seq, d), new_ret, new_buf, wts, tuple(ffn1_w), tuple(ffn2_w)


def kernel(x_prompt, x_sample, state_retention, state_conv, norm_ffn1_g, ffn1_w1, ffn1_w3, ffn1_w2, norm_mix_g, w_in, ret_gn_g, ret_gn_b, conv_w, conv_b, conv_ln_g, conv_ln_b, w_out, norm_ffn2_g, ffn2_w1, ffn2_w3, ffn2_w2, norm_final_g):
    depth = w_in.shape[0]
    assert depth == 1
    layer = 0
    row = lambda g: g.reshape(1, -1)
    wts = {
        "norm_ffn1_g": row(norm_ffn1_g[layer]),
        "norm_mix_g": row(norm_mix_g[layer]),
        "w_in": w_in[layer],
        "ret_gn_g": row(ret_gn_g[layer]),
        "ret_gn_b": row(ret_gn_b[layer]),
        "conv_w": conv_w[layer],
        "conv_b": row(conv_b[layer]),
        "conv_ln_g": row(conv_ln_g[layer]),
        "conv_ln_b": row(conv_ln_b[layer]),
        "w_out": w_out[layer],
        "norm_ffn2_g": row(norm_ffn2_g[layer]),
        "norm_final_g": row(norm_final_g),
    }
    bp, tp, _ = x_prompt.shape
    bs, ts, _ = x_sample.shape
    n_heads = state_retention.shape[2]
    d_conv = state_conv.shape[-1]

    pos_p = jnp.arange(tp, dtype=F32)
    pos_s = PAST_LEN + jnp.arange(ts, dtype=F32)
    r0 = jnp.zeros((1, bp, n_heads, RET_DK, RET_DV), state_retention.dtype)
    c0 = jnp.zeros((1, bp, CONV_HIST, d_conv), x_prompt.dtype)

    ffn1_w = (ffn1_w1[layer], ffn1_w3[layer], ffn1_w2[layer])
    ffn2_w = (ffn2_w1[layer], ffn2_w3[layer], ffn2_w2[layer])
    y_s, ret_s, conv_s, wts, ffn1_w, ffn2_w = _stream(
        x_sample, pos_s, state_retention, state_conv, wts, ffn1_w, ffn2_w, layer=layer,
        emit_bf16=True, ffn_tm=1024, tf=256, tm=512, ret_nb=8, ret_nc=1, conv_bb=16, conv_rt=None)
    y_p, ret_p, conv_p, _, _, _ = _stream(
        x_prompt, pos_p, r0, c0, wts, ffn1_w, ffn2_w, layer=0,
        emit_bf16=False, ffn_tm=1024, tf=512, tm=512, ret_nb=1, ret_nc=4, conv_bb=None, conv_rt=128)
    return (y_p, y_s, ret_p, conv_p, ret_s, conv_s)
```
